```python
import jax
import jax.numpy as jnp
from jax import lax
import numpy as np

D_MODEL = 1024
BATCH = 8
SEQ = 4096
DEPTH = 2

HEAD_DIM = 64
H_NA = 4
H_DIL = 6
H_GQ = 6
H_GKV = 2
NA_W = H_NA * HEAD_DIM
DIL_W = H_DIL * HEAD_DIM
GQ_W = H_GQ * HEAD_DIM
GKV_W = H_GKV * HEAD_DIM
D_MIX = NA_W + DIL_W + GQ_W
D_IN = 3 * NA_W + 3 * DIL_W + GQ_W + 2 * GKV_W

GRID_W = 64
NA_KH_MAX = 8
NA_KW = 16
DIL_PATTERNS = ((128, 1), (512, 4), (2048, 16))
ROPE_THETA = 500000.0
ROPE_DIMS = HEAD_DIM // 4
AXIAL_THETA = 10000.0
Q_BLK = 128
D_FF = 2816
CONV_W = 3
EPS = 1e-6
NEG_INF = -1e30

kernel_name = 'hybrid_parallel_heads_encoder'


def rms_norm(x, g):
    xf = x.astype(jnp.float32)
    y = xf * lax.rsqrt(jnp.mean(xf * xf, axis=-1, keepdims=True) + EPS)
    return (y * g.astype(jnp.float32)).astype(x.dtype)


def rope_cos_sin(pos, dim, theta):
    inv = theta ** (-jnp.arange(0, dim, 2, dtype=jnp.float32) / dim)
    ang = pos.astype(jnp.float32)[:, None] * inv[None, :]
    return jnp.cos(ang), jnp.sin(ang)


def rotate(x, cos, sin):
    half = x.shape[-1] // 2
    x1 = x[..., :half].astype(jnp.float32)
    x2 = x[..., half:].astype(jnp.float32)
    c = cos[None, :, None, :]
    s = sin[None, :, None, :]
    return jnp.concatenate([x1 * c - x2 * s, x2 * c + x1 * s], axis=-1).astype(x.dtype)


def partial_rotary(x, cos, sin):
    return jnp.concatenate([rotate(x[..., :ROPE_DIMS], cos, sin), x[..., ROPE_DIMS:]], axis=-1)


def axial_rotary(x, cos_r, sin_r, cos_c, sin_c):
    half = x.shape[-1] // 2
    return jnp.concatenate([rotate(x[..., :half], cos_r, sin_r), rotate(x[..., half:], cos_c, sin_c)], axis=-1)


def neighbourhood_tables(S):
    rows = S // GRID_W
    kh = min(NA_KH_MAX, rows)
    t = jnp.arange(S)
    r = t // GRID_W
    c = t % GRID_W
    r0 = jnp.clip(r - kh // 2, 0, rows - kh)
    c0 = jnp.clip(c - NA_KW // 2, 0, GRID_W - NA_KW)
    kr = r0[:, None] + jnp.arange(kh)[None, :]
    kc = c0[:, None] + jnp.arange(NA_KW)[None, :]
    idx = (kr[:, :, None] * GRID_W + kc[:, None, :]).reshape(S, kh * NA_KW)
    dr = (kr - r[:, None] + NA_KH_MAX - 1)[:, :, None]
    dc = (kc - c[:, None] + NA_KW - 1)[:, None, :]
    return idx, dr, dc


def neighbourhood_attention(q, k, v, rpb, idx, dr, dc):
    B, S, H, hd = q.shape
    nk = idx.shape[1]
    nb = S // Q_BLK
    scale = hd ** -0.5
    bias = rpb[:, dr, dc].reshape(H, S, nk).astype(jnp.float32)
    qb = q.reshape(B, nb, Q_BLK, H, hd).transpose(1, 0, 2, 3, 4)
    idxb = idx.reshape(nb, Q_BLK, nk)
    biasb = bias.reshape(H, nb, Q_BLK, nk).transpose(1, 0, 2, 3)

    def block(args):
        qi, ii, bi = args
        kg = k[:, ii]
        vg = v[:, ii]
        s = jnp.einsum('bqhd,bqkhd->bhqk', qi, kg).astype(jnp.float32) * scale + bi[None]
        p = jax.nn.softmax(s, axis=-1).astype(v.dtype)
        return jnp.einsum('bhqk,bqkhd->bqhd', p, vg)

    o = lax.map(block, (qb, idxb, biasb))
    return o.transpose(1, 0, 2, 3, 4).reshape(B, S, H * hd)


def dilated_tables(S):
    t = jnp.arange(S)
    idxs, valids = [], []
    for w, d in DIL_PATTERNS:
        n = (w // 2) // d
        pos = t[:, None] + (jnp.arange(-n, n + 1) * d)[None, :]
        valids.append((pos >= 0) & (pos < S))
        idxs.append(jnp.clip(pos, 0, S - 1))
    return tuple(idxs), tuple(valids)


def dilated_window_attention(q, k, v, idxs, valids):
    B, S, H, hd = q.shape
    nb = S // Q_BLK
    scale = hd ** -0.5
    qb = q.reshape(B, nb, Q_BLK, H, hd).transpose(1, 0, 2, 3, 4)
    idxb = tuple(i.reshape(nb, Q_BLK, -1) for i in idxs)
    valb = tuple(m.reshape(nb, Q_BLK, -1) for m in valids)

    def block(args):
        qi, ii, mm = args
        lses, outs = [], []
        for ig, mg in zip(ii, mm):
            kg = k[:, ig]
            vg = v[:, ig]
            s = jnp.einsum('bqhd,bqkhd->bhqk', qi, kg).astype(jnp.float32) * scale
            s = jnp.where(mg[None, None], s, NEG_INF)
            lse = jax.nn.logsumexp(s, axis=-1)
            p = jnp.exp(s - lse[..., None]).astype(v.dtype)
            outs.append(jnp.einsum('bhqk,bqkhd->bqhd', p, vg))
            lses.append(lse)
        wts = jax.nn.softmax(jnp.stack(lses), axis=0).astype(v.dtype)
        return jnp.einsum('gbhq,gbqhd->bqhd', wts, jnp.stack(outs))

    o = lax.map(block, (qb, idxb, valb))
    return o.transpose(1, 0, 2, 3, 4).reshape(B, S, H * hd)


def gqa_block_attention(q, k, v):
    B, S, Hq, hd = q.shape
    Hkv = k.shape[2]
    g = Hq // Hkv
    nb = S // Q_BLK
    scale = hd ** -0.5
    qb = q.reshape(B, nb, Q_BLK, Hkv, g, hd).transpose(1, 0, 2, 3, 4, 5)

    def block(qi):
        s = jnp.einsum('bqkgd,bskd->bkgqs', qi, k).astype(jnp.float32) * scale
        p = jax.nn.softmax(s, axis=-1).astype(v.dtype)
        return jnp.einsum('bkgqs,bskd->bqkgd', p, v)

    o = lax.map(block, qb)
    return o.transpose(1, 0, 2, 3, 4, 5).reshape(B, S, Hq * hd)


def depthwise_conv_centred(z, w, b):
    S = z.shape[1]
    pad = CONV_W // 2
    zp = jnp.pad(z, ((0, 0), (pad, pad), (0, 0)))
    out = zp[:, 0:S] * w[0] + b
    for j in range(1, CONV_W):
        out = out + zp[:, j:j + S] * w[j]
    return out


def mixer_sublayer(x, n1, w_in_l, qg, kg, rpb_l, og, w_out_l, tables):
    (na_idx, na_dr, na_dc, dil_idx, dil_val, cos1, sin1, cos_r, sin_r, cos_c, sin_c) = tables
    B, S, _ = x.shape
    h = rms_norm(x, n1)
    proj = h @ w_in_l
    sizes = (NA_W, NA_W, NA_W, DIL_W, DIL_W, DIL_W, GQ_W, GKV_W, GKV_W)
    cuts = [int(c) for c in np.cumsum(sizes)[:-1]]
    qa, ka, va, qd, kd, vd, qc, kc, vc = jnp.split(proj, cuts, axis=-1)

    def heads(z, n):
        return z.reshape(B, S, n, HEAD_DIM)

    qa = rms_norm(heads(qa, H_NA), qg[0])
    ka = rms_norm(heads(ka, H_NA), kg[0])
    out_a = neighbourhood_attention(qa, ka, heads(va, H_NA), rpb_l, na_idx, na_dr, na_dc)
    qd = partial_rotary(rms_norm(heads(qd, H_DIL), qg[1]), cos1, sin1)
    kd = partial_rotary(rms_norm(heads(kd, H_DIL), kg[1]), cos1, sin1)
    out_d = dilated_window_attention(qd, kd, heads(vd, H_DIL), dil_idx, dil_val)
    qc = axial_rotary(rms_norm(heads(qc, H_GQ), qg[2]), cos_r, sin_r, cos_c, sin_c)
    kc = axial_rotary(rms_norm(heads(kc, H_GKV), kg[2]), cos_r, sin_r, cos_c, sin_c)
    out_c = gqa_block_attention(qc, kc, heads(vc, H_GKV))

    mix = jnp.concatenate([
        rms_norm(out_a, og[:NA_W]),
        rms_norm(out_d, og[NA_W:NA_W + DIL_W]),
        rms_norm(out_c, og[NA_W + DIL_W:]),
    ], axis=-1)
    return x + mix @ w_out_l


def channel_sublayer(x, n2, w_gu, cw, cb, w_dn):
    h = rms_norm(x, n2)
    g, u = jnp.split(h @ w_gu, 2, axis=-1)
    g = depthwise_conv_centred(g, cw, cb)
    return x + (jax.nn.gelu(g, approximate=False) * u) @ w_dn


def setup_inputs(seed: int = 0) -> dict:
    key = jax.random.key(seed)
    ks = jax.random.split(key, 13)

    def nrm(k, shape, s):
        return s * jax.random.normal(k, shape, jnp.float32)

    return {
        'x': nrm(ks[0], (BATCH, SEQ, D_MODEL), 1.0),
        'norm1_g': 1.0 + nrm(ks[1], (DEPTH, D_MODEL), 0.02),
        'w_in': nrm(ks[2], (DEPTH, D_MODEL, D_IN), D_MODEL ** -0.5),
        'q_norm_g': 1.0 + nrm(ks[3], (DEPTH, 3, HEAD_DIM), 0.02),
        'k_norm_g': 1.0 + nrm(ks[4], (DEPTH, 3, HEAD_DIM), 0.02),
        'rpb': nrm(ks[5], (DEPTH, H_NA, 2 * NA_KH_MAX - 1, 2 * NA_KW - 1), 0.1),
        'out_norm_g': 1.0 + nrm(ks[6], (DEPTH, D_MIX), 0.02),
        'w_out': nrm(ks[7], (DEPTH, D_MIX, D_MODEL), D_MIX ** -0.5),
        'norm2_g': 1.0 + nrm(ks[8], (DEPTH, D_MODEL), 0.02),
        'w_gate_up': nrm(ks[9], (DEPTH, D_MODEL, 2 * D_FF), D_MODEL ** -0.5),
        'conv_w': nrm(ks[10], (DEPTH, CONV_W, D_FF), CONV_W ** -0.5),
        'conv_b': nrm(ks[11], (DEPTH, D_FF), 0.02),
        'w_down': nrm(ks[12], (DEPTH, D_FF, D_MODEL), D_FF ** -0.5),
    }


def reference(x, norm1_g, w_in, q_norm_g, k_norm_g, rpb, out_norm_g, w_out, norm2_g, w_gate_up, conv_w, conv_b, w_down):
    S = x.shape[1]
    t = jnp.arange(S, dtype=jnp.int32)
    cos1, sin1 = rope_cos_sin(t, ROPE_DIMS, ROPE_THETA)
    cos_r, sin_r = rope_cos_sin(t // GRID_W, HEAD_DIM // 2, AXIAL_THETA)
    cos_c, sin_c = rope_cos_sin(t % GRID_W, HEAD_DIM // 2, AXIAL_THETA)
    na_idx, na_dr, na_dc = neighbourhood_tables(S)
    dil_idx, dil_val = dilated_tables(S)
    tables = (na_idx, na_dr, na_dc, dil_idx, dil_val, cos1, sin1, cos_r, sin_r, cos_c, sin_c)
    for l in range(DEPTH):
        x = mixer_sublayer(x, norm1_g[l], w_in[l], q_norm_g[l], k_norm_g[l], rpb[l],
                           out_norm_g[l], w_out[l], tables)
        x = channel_sublayer(x, norm2_g[l], w_gate_up[l], conv_w[l], conv_b[l], w_down[l])
    return x
```

```python
import functools

import jax
import jax.numpy as jnp
from jax import lax
from jax.experimental import pallas as pl
from jax.experimental.pallas import tpu as pltpu

BF = jnp.bfloat16
F32 = jnp.float32

HEAD_DIM = 64
H_NA, H_DIL, H_GQ, H_GKV = 4, 6, 6, 2
NA_W, DIL_W, GQ_W, GKV_W = 256, 384, 384, 128
GRID_W = 64
NA_KH = 8
NA_KW = 16
DIL_PATTERNS = ((128, 1), (512, 4), (2048, 16))
BAND_HALF = 64
ROPE_THETA = 500000.0
ROPE_DIMS = 16
AXIAL_THETA = 10000.0
CONV_W = 3
EPS = 1e-6
NEG_INF = -1e30

LANES = 128
MXU_N = 256
VMEM_LIMIT = 56 * 1024 * 1024

TM_PROJ = 512
TM_FFN = 512
FFN_HALO = 16
TQ_BAND = 128
BAND_WIN = 256
TQ_GQA = 256
TK_GQA = 512


def _cparams(n_axes):
    return pltpu.CompilerParams(
        dimension_semantics=("arbitrary",) * n_axes, vmem_limit_bytes=VMEM_LIMIT)


def _rms(v, g):
    ms = jnp.mean(v * v, axis=-1, keepdims=True)
    return v * lax.rsqrt(ms + EPS) * g


def _resident(shape):
    nd = len(shape)
    return pl.BlockSpec(shape, lambda *_: (0,) * nd, pipeline_mode=pl.Buffered(1))


def _rotary(p, cos, sin, first, half):
    up = pltpu.roll(p, LANES - half, axis=1)
    dn = pltpu.roll(p, half, axis=1)
    return p * cos + jnp.where(first, up, dn) * sin


def _inproj_body(x_ref, g1_ref, w_ref, gs_ref, gain_ref, cb_ref, sb_ref, cc_ref, sc_ref,
                 qa_ref, ka_ref, va_ref, qd_ref, kd_ref, vd_ref, qc_ref, kvc_ref):
    h = _rms(x_ref[...], g1_ref[...]).astype(BF)
    lane = lax.broadcasted_iota(jnp.int32, (1, LANES), 1)
    first_b = (lane % HEAD_DIM) < (ROPE_DIMS // 2)
    first_c = (lane % (HEAD_DIM // 2)) < (HEAD_DIM // 4)
    dests = (
        ((qa_ref, 0), (qa_ref, 128)),
        ((ka_ref, 0), (ka_ref, 128)),
        ((qd_ref, 0), (qd_ref, 128)),
        ((qd_ref, 256), (kd_ref, 0)),
        ((kd_ref, 128), (kd_ref, 256)),
        ((qc_ref, 0), (qc_ref, 128)),
        ((qc_ref, 256), (kvc_ref, 0)),
        ((va_ref, 0), (va_ref, 128)),
        ((vd_ref, 0), (vd_ref, 128)),
        ((vd_ref, 256), (kvc_ref, 128)),
    )
    for c in range(10):
        cols = slice(c * MXU_N, (c + 1) * MXU_N)
        p = jnp.dot(h, w_ref[:, cols], preferred_element_type=F32)
        if c < 7:
            ss = jnp.dot((p * p).astype(BF), gs_ref[...], preferred_element_type=F32)
            p = p * lax.rsqrt(ss * (1.0 / HEAD_DIM) + EPS) * gain_ref[:, cols]
        halves = [p[:, :LANES], p[:, LANES:]]
        if 2 <= c <= 4:
            halves = [_rotary(v, cb_ref[...], sb_ref[...], first_b, ROPE_DIMS // 2) for v in halves]
        elif 5 <= c <= 6:
            halves = [_rotary(v, cc_ref[...], sc_ref[...], first_c, HEAD_DIM // 4) for v in halves]
        for v, (ref, off) in zip(halves, dests[c]):
            ref[:, off:off + LANES] = v.astype(BF)


def _inproj(x2, g1, w_perm, gsum, gain, tabs, S):
    N, D = x2.shape
    tm = TM_PROJ
    nt_seq = S // tm
    row = lambda i: (i, 0)
    tab = pl.BlockSpec((tm, LANES), lambda i: (i % nt_seq, 0))
    widths = (NA_W, NA_W, NA_W, DIL_W, DIL_W, DIL_W, GQ_W, 2 * GKV_W)
    return pl.pallas_call(
        _inproj_body,
        grid=(N // tm,),
        in_specs=[
            pl.BlockSpec((tm, D), row),
            _resident(g1.shape),
            _resident(w_perm.shape),
            _resident(gsum.shape),
            _resident(gain.shape),
            tab, tab, tab, tab,
        ],
        out_specs=[pl.BlockSpec((tm, w), row) for w in widths],
        out_shape=[jax.ShapeDtypeStruct((N, w), BF) for w in widths],
        compiler_params=_cparams(1),
        name="inproj",
    )(x2, g1, w_perm, gsum, gain, *tabs)


def _band_body(*refs, first, final, L):
    q_ref, k_ref, v_ref = refs[:3]
    if first:
        m_out, l_out, acc_out = refs[3:]
    elif final:
        m_in, l_in, acc_in, o_ref = refs[3:]
    else:
        m_in, l_in, acc_in, m_out, l_out, acc_out = refs[3:]
    tq, win = TQ_BAND, BAND_WIN
    i0 = pl.program_id(2) * tq
    start = pl.multiple_of(jnp.clip(i0 - BAND_HALF, 0, L - win), BAND_HALF)
    qi = i0 + lax.broadcasted_iota(jnp.int32, (tq, win), 0)
    ki = start + lax.broadcasted_iota(jnp.int32, (tq, win), 1)
    band = jnp.abs(qi - ki) <= BAND_HALF
    lane = lax.broadcasted_iota(jnp.int32, (tq, LANES), 1)
    lo = lane < HEAD_DIM
    m_blk = jnp.zeros((tq, LANES), F32)
    l_blk = jnp.zeros((tq, LANES), F32)
    for hp in range(H_DIL // 2):
        cols = slice(hp * LANES, (hp + 1) * LANES)
        qp = q_ref[0, :, cols]
        kw = k_ref[0, pl.ds(start, win), cols]
        vw = v_ref[0, pl.ds(start, win), cols]
        pv, alpha, linv = [], [], []
        for hh in range(2):
            h = 2 * hp + hh
            qm = jnp.where(lo if hh == 0 else jnp.logical_not(lo), qp, jnp.zeros_like(qp))
            s = lax.dot_general(qm, kw, (((1,), (1,)), ((), ())), preferred_element_type=F32)
            s = jnp.where(band, s, NEG_INF)
            m_new = jnp.max(s, axis=1, keepdims=True)
            if not first:
                m_prev = m_in[0, :, h:h + 1]
                m_new = jnp.maximum(m_new, m_prev)
            p = jnp.exp(s - m_new)
            l_new = jnp.sum(p, axis=1, keepdims=True)
            if not first:
                a = jnp.exp(m_prev - m_new)
                l_new = l_new + a * l_in[0, :, h:h + 1]
                alpha.append(a)
            pv.append(jnp.dot(p.astype(BF), vw, preferred_element_type=F32))
            if final:
                linv.append(1.0 / l_new)
            else:
                m_blk = jnp.where(lane == h, m_new, m_blk)
                l_blk = jnp.where(lane == h, l_new, l_blk)
        acc = jnp.where(lo, pv[0], pv[1])
        if not first:
            acc = acc + jnp.where(lo, alpha[0], alpha[1]) * acc_in[0, :, cols]
        if final:
            o_ref[0, :, cols] = (acc * jnp.where(lo, linv[0], linv[1])).astype(BF)
        else:
            acc_out[0, :, cols] = acc
    if not final:
        m_out[0] = m_blk
        l_out[0] = l_blk


def _band_call(q, k, v, state, d, first, final):
    B, S, W = q.shape
    L = S // d
    tq = TQ_BAND
    view = lambda a: a.reshape(B, L, d * a.shape[-1])
    qblk = pl.BlockSpec((1, tq, W), lambda b, c, j: (b, j, c))
    kblk = pl.BlockSpec((1, L, W), lambda b, c, j: (b, 0, c))
    sblk = pl.BlockSpec((1, tq, LANES), lambda b, c, j: (b, j, c))
    in_specs = [qblk, kblk, kblk]
    args = [view(q), view(k), view(v)]
    if not first:
        in_specs += [sblk, sblk, qblk]
        args += [view(a) for a in state]
    if final:
        out_specs = qblk
        out_shape = jax.ShapeDtypeStruct((B, L, d * W), BF)
    else:
        out_specs = [sblk, sblk, qblk]
        out_shape = [jax.ShapeDtypeStruct((B, L, d * LANES), F32)] * 2 + [
            jax.ShapeDtypeStruct((B, L, d * W), F32)]
    out = pl.pallas_call(
        functools.partial(_band_body, first=first, final=final, L=L),
        grid=(B, d, L // tq),
        in_specs=in_specs,
        out_specs=out_specs,
        out_shape=out_shape,
        compiler_params=_cparams(3),
        name=f"dilated_d{d}",
    )(*args)
    if final:
        return out.reshape(B, S, W)
    return tuple(a.reshape(B, S, a.shape[-1] // d) for a in out)


def _dilated_attention(qd, kd, vd):
    state = None
    n = len(DIL_PATTERNS)
    for g, (w, d) in enumerate(DIL_PATTERNS):
        assert (w // 2) // d == BAND_HALF
        state = _band_call(qd, kd, vd, state, d, first=(g == 0), final=(g == n - 1))
    return state


NA_ROWS_WIN = 10
NA_DR = 2 * NA_KH - 1
NA_DC = 2 * NA_KW - 1


def _bias2_body(rpb_ref, o_ref):
    h = pl.program_id(0)
    d = pl.program_id(1)
    cq = lax.broadcasted_iota(jnp.int32, (GRID_W, LANES), 0)
    lane = lax.broadcasted_iota(jnp.int32, (GRID_W, LANES), 1)
    kc = lane % GRID_W
    half = lane // GRID_W
    dc = kc - cq + (NA_KW - 1)
    c0 = jnp.clip(cq - NA_KW // 2, 0, GRID_W - NA_KW)
    dr = d - 1 + half
    ok = (kc >= c0) & (kc < c0 + NA_KW) & (dr >= 0) & (dr < NA_DR)
    val = jnp.zeros((GRID_W, LANES), F32)
    for hf in range(2):
        base = (h * NA_DR + jnp.clip(d - 1 + hf, 0, NA_DR - 1)) * NA_DC
        for j in range(NA_DC):
            val = jnp.where((dc == j) & (half == hf), rpb_ref[base + j], val)
    o_ref[0, 0] = jnp.where(ok, val, NEG_INF)


def _bias2(rpb_l):
    return pl.pallas_call(
        _bias2_body,
        grid=(H_NA, NA_DR + 1),
        in_specs=[pl.BlockSpec(memory_space=pltpu.SMEM)],
        out_specs=pl.BlockSpec((1, 1, GRID_W, LANES), lambda h, d: (h, d, 0, 0)),
        out_shape=jax.ShapeDtypeStruct((H_NA, NA_DR + 1, GRID_W, LANES), F32),
        compiler_params=_cparams(2),
        name="nbr_bias",
    )(rpb_l.reshape(-1))


def _nbr_body(q_ref, k_ref, v_ref, b2_ref, o_ref, *, rows):
    j = pl.program_id(1)
    rs = jnp.clip(2 * j - NA_KH // 2, 0, rows - NA_ROWS_WIN)
    start = pl.multiple_of(rs * GRID_W, 2 * GRID_W)
    win = NA_ROWS_WIN * GRID_W
    lane_b = lax.broadcasted_iota(jnp.int32, (GRID_W, LANES), 1) // GRID_W
    lane = lax.broadcasted_iota(jnp.int32, (TQ_BAND, LANES), 1)
    lo = lane < HEAD_DIM
    for hp in range(H_NA // 2):
        cols = slice(hp * LANES, (hp + 1) * LANES)
        qp = q_ref[0, :, cols]
        kw = k_ref[0, pl.ds(start, win), cols]
        vw = v_ref[0, pl.ds(start, win), cols]
        outs = []
        for hh in range(2):
            h = 2 * hp + hh
            qm = jnp.where(lo if hh == 0 else jnp.logical_not(lo), qp, jnp.zeros_like(qp))
            s = lax.dot_general(qm, kw, (((1,), (1,)), ((), ())), preferred_element_type=F32)
            bias_rows = []
            for qi in range(2):
                rq = 2 * j + qi
                r0 = jnp.clip(rq - NA_KH // 2, 0, rows - NA_KH)
                blocks = []
                for kk in range(NA_ROWS_WIN // 2):
                    kr = rs + 2 * kk
                    d = jnp.clip(kr - rq + NA_KH, 0, NA_DR)
                    krl = kr + lane_b
                    ok = (krl >= r0) & (krl < r0 + NA_KH)
                    blocks.append(jnp.where(ok, b2_ref[h, d], NEG_INF))
                bias_rows.append(jnp.concatenate(blocks, axis=1))
            s = s + jnp.concatenate(bias_rows, axis=0)
            m = jnp.max(s, axis=1, keepdims=True)
            p = jnp.exp(s - m)
            linv = 1.0 / jnp.sum(p, axis=1, keepdims=True)
            outs.append(jnp.dot(p.astype(BF), vw, preferred_element_type=F32) * linv)
        o_ref[0, :, cols] = jnp.where(lo, outs[0], outs[1]).astype(BF)


def _nbr_attention(qa, ka, va, bias2):
    B, S, W = qa.shape
    rows = S // GRID_W
    assert rows >= NA_ROWS_WIN and S % TQ_BAND == 0
    qblk = pl.BlockSpec((1, TQ_BAND, W), lambda b, j: (b, j, 0))
    kblk = pl.BlockSpec((1, S, W), lambda b, j: (b, 0, 0))
    return pl.pallas_call(
        functools.partial(_nbr_body, rows=rows),
        grid=(B, S // TQ_BAND),
        in_specs=[qblk, kblk, kblk, _resident(bias2.shape)],
        out_specs=qblk,
        out_shape=jax.ShapeDtypeStruct((B, S, W), BF),
        compiler_params=_cparams(2),
        name="nbr_attn",
    )(qa, ka, va, bias2)


def _gqa_body(q_ref, kv_ref, o_ref, *, S):
    tq, tk = TQ_GQA, TK_GQA
    g = H_GQ // H_GKV
    lane = lax.broadcasted_iota(jnp.int32, (tq, LANES), 1)
    lo = lane < HEAD_DIM
    res = []
    for kvh in range(H_GKV):
        mine = lo if kvh == 0 else jnp.logical_not(lo)
        parts = []
        for i in range(g):
            h = g * kvh + i
            chunk = q_ref[0, :, (h // 2) * LANES:(h // 2 + 1) * LANES]
            if (h % 2) != kvh:
                chunk = pltpu.roll(chunk.astype(F32), HEAD_DIM, axis=1).astype(BF)
            parts.append(jnp.where(mine, chunk, jnp.zeros_like(chunk)))
        qs = jnp.concatenate(parts, axis=0)

        def step(c, carry):
            m, l, acc = carry
            rows = pl.ds(pl.multiple_of(c * tk, tk), tk)
            kblk = kv_ref[0, rows, 0:LANES]
            vblk = kv_ref[0, rows, LANES:2 * LANES]
            s = lax.dot_general(qs, kblk, (((1,), (1,)), ((), ())), preferred_element_type=F32)
            m_new = jnp.maximum(m, jnp.max(s, axis=1, keepdims=True))
            p = jnp.exp(s - m_new)
            a = jnp.exp(m - m_new)
            l = a * l + jnp.sum(p, axis=1, keepdims=True)
            acc = a * acc + jnp.dot(p.astype(BF), vblk, preferred_element_type=F32)
            return m_new, l, acc

        init = (jnp.full((g * tq, 1), NEG_INF, F32), jnp.zeros((g * tq, 1), F32),
                jnp.zeros((g * tq, LANES), F32))
        m, l, acc = lax.fori_loop(0, S // tk, step, init)
        res.append(acc * (1.0 / l))
    for c in range(GQ_W // LANES):
        halves = []
        for hh in range(2):
            h = 2 * c + hh
            kvh, i = h // g, h % g
            v = res[kvh][i * tq:(i + 1) * tq]
            if kvh != hh:
                v = pltpu.roll(v, HEAD_DIM, axis=1)
            halves.append(v)
        o_ref[0, :, c * LANES:(c + 1) * LANES] = jnp.where(lo, halves[0], halves[1]).astype(BF)


def _gqa_attention(qc, kvc):
    B, S, W = qc.shape
    qblk = pl.BlockSpec((1, TQ_GQA, W), lambda b, j: (b, j, 0))
    kvblk = pl.BlockSpec((1, S, 2 * GKV_W), lambda b, j: (b, 0, 0))
    return pl.pallas_call(
        functools.partial(_gqa_body, S=S),
        grid=(B, S // TQ_GQA),
        in_specs=[qblk, kvblk],
        out_specs=qblk,
        out_shape=jax.ShapeDtypeStruct((B, S, W), BF),
        compiler_params=_cparams(2),
        name="gqa_attn",
    )(qc, kvc)


def _outproj_body(oa_ref, od_ref, oc_ref, og_ref, x_ref, w_ref, y_ref):
    na = _rms(oa_ref[...].astype(F32), og_ref[:, :NA_W])
    nd = _rms(od_ref[...].astype(F32), og_ref[:, NA_W:NA_W + DIL_W])
    nc = _rms(oc_ref[...].astype(F32), og_ref[:, NA_W + DIL_W:])
    mix = jnp.concatenate([na, nd, nc], axis=1).astype(BF)
    y_ref[...] = x_ref[...] + jnp.dot(mix, w_ref[...], preferred_element_type=F32)


def _outproj(oa, od, oc, og, x2, w_out):
    N, D = x2.shape
    tm = TM_PROJ
    row = lambda i: (i, 0)
    return pl.pallas_call(
        _outproj_body,
        grid=(N // tm,),
        in_specs=[
            pl.BlockSpec((tm, NA_W), row), pl.BlockSpec((tm, DIL_W), row),
            pl.BlockSpec((tm, GQ_W), row), _resident(og.shape),
            pl.BlockSpec((tm, D), row), _resident(w_out.shape),
        ],
        out_specs=pl.BlockSpec((tm, D), row),
        out_shape=jax.ShapeDtypeStruct((N, D), F32),
        compiler_params=_cparams(1),
        name="outproj",
    )(oa, od, oc, og, x2, w_out)


def _ffn_body(y_ref, yp_ref, yn_ref, g2_ref, wgu_ref, cw_ref, cb_ref, wd_ref, o_ref,
              hext, gs, act, *, tiles_per_seq, d_ff):
    tm, halo = TM_FFN, FFN_HALO
    i = pl.program_id(0) % tiles_per_seq
    g2 = g2_ref[...]
    y = y_ref[...]
    prev_ok = (i > 0).astype(F32)
    next_ok = (i < tiles_per_seq - 1).astype(F32)
    hext[0:halo] = (_rms(yp_ref[...], g2) * prev_ok).astype(BF)
    hext[halo:halo + tm] = _rms(y, g2).astype(BF)
    hext[halo + tm:] = (_rms(yn_ref[...], g2) * next_ok).astype(BF)
    for c in range(d_ff // MXU_N):
        cols = slice(c * MXU_N, (c + 1) * MXU_N)
        ucols = slice(d_ff + c * MXU_N, d_ff + (c + 1) * MXU_N)
        gs[...] = jnp.dot(hext[...], wgu_ref[:, cols], preferred_element_type=F32)
        u = jnp.dot(hext[halo:halo + tm], wgu_ref[:, ucols], preferred_element_type=F32)
        gc = gs[halo - 1:halo - 1 + tm] * cw_ref[0:1, cols] + cb_ref[:, cols]
        gc = gc + gs[halo:halo + tm] * cw_ref[1:2, cols]
        gc = gc + gs[halo + 1:halo + 1 + tm] * cw_ref[2:3, cols]
        gelu = 0.5 * gc * (1.0 + lax.erf(gc * (0.5 ** 0.5)))
        act[:, cols] = (gelu * u).astype(BF)
    o_ref[...] = y + jnp.dot(act[...], wd_ref[...], preferred_element_type=F32)


def _ffn(y2, g2, wgu, cw, cb, wd, S):
    N, D = y2.shape
    d_ff = wd.shape[0]
    tm, halo = TM_FFN, FFN_HALO
    assert S % tm == 0 and d_ff % MXU_N == 0
    r = tm // halo
    nblk = N // halo
    row = lambda i: (i, 0)
    return pl.pallas_call(
        functools.partial(_ffn_body, tiles_per_seq=S // tm, d_ff=d_ff),
        grid=(N // tm,),
        in_specs=[
            pl.BlockSpec((tm, D), row),
            pl.BlockSpec((halo, D), lambda i: (jnp.maximum(i * r - 1, 0), 0)),
            pl.BlockSpec((halo, D), lambda i: (jnp.minimum((i + 1) * r, nblk - 1), 0)),
            _resident(g2.shape), _resident(wgu.shape), _resident(cw.shape),
            _resident(cb.shape), _resident(wd.shape),
        ],
        out_specs=pl.BlockSpec((tm, D), row),
        out_shape=jax.ShapeDtypeStruct((N, D), F32),
        scratch_shapes=[
            pltpu.VMEM((tm + 2 * halo, D), BF),
            pltpu.VMEM((tm + 2 * halo, MXU_N), F32),
            pltpu.VMEM((tm, d_ff), BF),
        ],
        compiler_params=_cparams(1),
        name="ffn",
    )(y2, y2, y2, g2, wgu, cw, cb, wd)


def _rope_cos_sin(pos, dim, theta):
    inv = theta ** (-jnp.arange(0, dim, 2, dtype=F32) / dim)
    ang = pos.astype(F32)[:, None] * inv[None, :]
    return jnp.cos(ang), jnp.sin(ang)


def _rotary_tables(S):
    t = jnp.arange(S, dtype=jnp.int32)
    cos1, sin1 = _rope_cos_sin(t, ROPE_DIMS, ROPE_THETA)
    rest = HEAD_DIM - ROPE_DIMS
    cb = jnp.concatenate([cos1, cos1, jnp.ones((S, rest), F32)], axis=1)
    sb = jnp.concatenate([-sin1, sin1, jnp.zeros((S, rest), F32)], axis=1)
    cr, sr = _rope_cos_sin(t // GRID_W, HEAD_DIM // 2, AXIAL_THETA)
    cc, sc = _rope_cos_sin(t % GRID_W, HEAD_DIM // 2, AXIAL_THETA)
    ca = jnp.concatenate([cr, cr, cc, cc], axis=1)
    sa = jnp.concatenate([-sr, sr, -sc, sc], axis=1)
    two = lambda a: jnp.concatenate([a, a], axis=1)
    return two(cb), two(sb), two(ca), two(sa)


def _permute_in_columns(w):
    sizes = (NA_W, NA_W, NA_W, DIL_W, DIL_W, DIL_W, GQ_W, GKV_W, GKV_W)
    offs = [0]
    for s in sizes:
        offs.append(offs[-1] + s)
    seg = [w[:, offs[i]:offs[i + 1]] for i in range(len(sizes))]
    qa, ka, va, qd, kd, vd, qc, kc, vc = seg
    return jnp.concatenate([qa, ka, qd, kd, qc, kc, va, vd, vc], axis=1)


def _gain_vector(qg, kg):
    scale = HEAD_DIM ** -0.5
    parts = [
        jnp.tile(qg[0] * scale, H_NA), jnp.tile(kg[0], H_NA),
        jnp.tile(qg[1] * scale, H_DIL), jnp.tile(kg[1], H_DIL),
        jnp.tile(qg[2] * scale, H_GQ), jnp.tile(kg[2], H_GKV),
    ]
    return jnp.concatenate(parts)[None, :]


def kernel(x, norm1_g, w_in, q_norm_g, k_norm_g, rpb, out_norm_g, w_out, norm2_g,
           w_gate_up, conv_w, conv_b, w_down):
    B, S, D = x.shape
    depth = w_in.shape[0]
    N = B * S
    assert S == GRID_W * GRID_W and S % TM_PROJ == 0 and S % TQ_GQA == 0 and S % TK_GQA == 0
    tabs = _rotary_tables(S)
    idx = jnp.arange(MXU_N) // HEAD_DIM
    gsum = (idx[:, None] == idx[None, :]).astype(BF)
    x2 = x.reshape(N, D)
    for l in range(depth):
        w_perm = _permute_in_columns(w_in[l].astype(BF))
        gain = _gain_vector(q_norm_g[l], k_norm_g[l])
        qa, ka, va, qd, kd, vd, qc, kvc = _inproj(
            x2, norm1_g[l][None, :], w_perm, gsum, gain, tabs, S)
        seq = lambda a: a.reshape(B, S, a.shape[-1])
        out_a = _nbr_attention(seq(qa), seq(ka), seq(va), _bias2(rpb[l]))
        out_d = _dilated_attention(seq(qd), seq(kd), seq(vd))
        out_c = _gqa_attention(seq(qc), seq(kvc))
        y2 = _outproj(out_a.reshape(N, NA_W), out_d.reshape(N, DIL_W), out_c.reshape(N, GQ_W),
                      out_norm_g[l][None, :], x2, w_out[l].astype(BF))
        x2 = _ffn(y2, norm2_g[l][None, :], w_gate_up[l].astype(BF), conv_w[l],
                  conv_b[l][None, :], w_down[l].astype(BF), S)
    return x2.reshape(B, S, D)
```

```python
import functools

import jax
import jax.numpy as jnp
from jax import lax
from jax.experimental import pallas as pl
from jax.experimental.pallas import tpu as pltpu

BF = jnp.bfloat16
F32 = jnp.float32

HEAD_DIM = 64
H_NA, H_DIL, H_GQ, H_GKV = 4, 6, 6, 2
NA_W, DIL_W, GQ_W, GKV_W = 256, 384, 384, 128
GRID_W = 64
NA_KH = 8
NA_KW = 16
DIL_PATTERNS = ((128, 1), (512, 4), (2048, 16))
BAND_HALF = 64
ROPE_THETA = 500000.0
ROPE_DIMS = 16
AXIAL_THETA = 10000.0
CONV_W = 3
EPS = 1e-6
NEG_INF = -1e30
LOG2E = 1.4426950408889634

LANES = 128
MXU_N = 256
VMEM_LIMIT = 56 * 1024 * 1024

TM_PROJ = 512
TM_FFN = 512
FFN_HALO = 16
TQ_BAND = 128
BAND_WIN = 256
TQ_GQA = 256
TK_GQA = 512


def _cparams(n_axes):
    return pltpu.CompilerParams(
        dimension_semantics=("arbitrary",) * n_axes, vmem_limit_bytes=VMEM_LIMIT)


def _rms(v, g):
    ms = jnp.mean(v * v, axis=-1, keepdims=True)
    return v * lax.rsqrt(ms + EPS) * g


def _resident(shape):
    nd = len(shape)
    return pl.BlockSpec(shape, lambda *_: (0,) * nd, pipeline_mode=pl.Buffered(1))


def _rotary(p, cos, sin, first, half):
    up = pltpu.roll(p, LANES - half, axis=1)
    dn = pltpu.roll(p, half, axis=1)
    return p * cos + jnp.where(first, up, dn) * sin


def _inproj_body(x_ref, g1_ref, w_ref, gs_ref, gain_ref, cb_ref, sb_ref, cc_ref, sc_ref,
                 qa_ref, ka_ref, va_ref, qd_ref, kd_ref, vd_ref, qc_ref, kvc_ref):
    h = _rms(x_ref[...], g1_ref[...]).astype(BF)
    lane = lax.broadcasted_iota(jnp.int32, (1, LANES), 1)
    first_b = (lane % HEAD_DIM) < (ROPE_DIMS // 2)
    first_c = (lane % (HEAD_DIM // 2)) < (HEAD_DIM // 4)
    dests = (
        ((qa_ref, 0), (qa_ref, 128)),
        ((ka_ref, 0), (ka_ref, 128)),
        ((qd_ref, 0), (qd_ref, 128)),
        ((qd_ref, 256), (kd_ref, 0)),
        ((kd_ref, 128), (kd_ref, 256)),
        ((qc_ref, 0), (qc_ref, 128)),
        ((qc_ref, 256), (kvc_ref, 0)),
        ((va_ref, 0), (va_ref, 128)),
        ((vd_ref, 0), (vd_ref, 128)),
        ((vd_ref, 256), (kvc_ref, 128)),
    )
    for c in range(10):
        cols = slice(c * MXU_N, (c + 1) * MXU_N)
        p = jnp.dot(h, w_ref[:, cols], preferred_element_type=F32)
        if c < 7:
            ss = jnp.dot((p * p).astype(BF), gs_ref[...], preferred_element_type=F32)
            p = p * lax.rsqrt(ss * (1.0 / HEAD_DIM) + EPS) * gain_ref[:, cols]
        halves = [p[:, :LANES], p[:, LANES:]]
        if 2 <= c <= 4:
            halves = [_rotary(v, cb_ref[...], sb_ref[...], first_b, ROPE_DIMS // 2) for v in halves]
        elif 5 <= c <= 6:
            halves = [_rotary(v, cc_ref[...], sc_ref[...], first_c, HEAD_DIM // 4) for v in halves]
        for v, (ref, off) in zip(halves, dests[c]):
            ref[:, off:off + LANES] = v.astype(BF)


def _inproj(x2, g1, w_perm, gsum, gain, tabs, S):
    N, D = x2.shape
    tm = TM_PROJ
    nt_seq = S // tm
    row = lambda i: (i, 0)
    tab = pl.BlockSpec((tm, LANES), lambda i: (i % nt_seq, 0))
    widths = (NA_W, NA_W, NA_W, DIL_W, DIL_W, DIL_W, GQ_W, 2 * GKV_W)
    return pl.pallas_call(
        _inproj_body,
        grid=(N // tm,),
        in_specs=[
            pl.BlockSpec((tm, D), row),
            _resident(g1.shape),
            _resident(w_perm.shape),
            _resident(gsum.shape),
            _resident(gain.shape),
            tab, tab, tab, tab,
        ],
        out_specs=[pl.BlockSpec((tm, w), row) for w in widths],
        out_shape=[jax.ShapeDtypeStruct((N, w), BF) for w in widths],
        compiler_params=_cparams(1),
        name="inproj",
    )(x2, g1, w_perm, gsum, gain, *tabs)


DIL_STRIDES = tuple(d for _, d in DIL_PATTERNS)


def _band_bias():
    r = jnp.arange(TQ_BAND)[:, None]
    c = jnp.arange(BAND_WIN)[None, :]
    masks = [jnp.where(jnp.abs(r + off - c) <= BAND_HALF, 0.0, NEG_INF).astype(F32)
             for off in (0, BAND_HALF, 2 * BAND_HALF)]
    return jnp.stack([jnp.concatenate([m, m], axis=0) for m in masks])


def _band_tile(q, kwin, vwin_a, vwin_b, bias, lo):
    tq = TQ_BAND
    zero = jnp.zeros_like(q)
    qs = jnp.concatenate([jnp.where(lo, q, zero), jnp.where(lo, zero, q)], axis=0)
    s = lax.dot_general(qs, kwin, (((1,), (1,)), ((), ())), preferred_element_type=F32) + bias
    m = jnp.max(s, axis=1, keepdims=True)
    p = jnp.exp2(s - m).astype(BF)
    acc_a = jnp.dot(p[:tq], vwin_a, preferred_element_type=F32)
    acc_b = jnp.dot(p[tq:], vwin_b, preferred_element_type=F32)
    m_rep = jnp.broadcast_to(m, (2 * tq, LANES))
    return (acc_a, acc_b), (m_rep[:tq], m_rep[tq:])


def _dil_body(q_ref, k_ref, v_ref, bias_ref, o_ref,
              stage, stage4, kc, vac, vbc, qc, acc, mx, *, S):
    tq, win = TQ_BAND, BAND_WIN
    half = S // 2
    hf = pl.program_id(2)
    lo = lax.broadcasted_iota(jnp.int32, (tq, LANES), 1) < HEAD_DIM

    def deinterleave(dst, g0, rows):
        n4, n16 = rows // 4, rows // 16
        for r in range(4):
            cls = stage[pl.ds(r, n4, stride=4), :]
            stage4[r * n4:(r + 1) * n4, :] = cls
            dst[g0, r * n4:(r + 1) * n4, :] = cls.astype(BF)
        for r in range(16):
            cls = stage4[pl.ds((r % 4) * n4 + r // 4, n16, stride=4), :]
            dst[g0 + 1, r * n16:(r + 1) * n16, :] = cls.astype(BF)

    @pl.when(hf == 0)
    def _():
        lo_s = lax.broadcasted_iota(jnp.int32, (S, LANES), 1) < HEAD_DIM
        stage[...] = k_ref[0].astype(F32)
        deinterleave(kc, 0, S)
        v = v_ref[0].astype(F32)
        stage[...] = jnp.where(lo_s, v, 1.0)
        vac[0] = stage[...].astype(BF)
        deinterleave(vac, 1, S)
        stage[...] = jnp.where(lo_s, 1.0, v)
        vbc[0] = stage[...].astype(BF)
        deinterleave(vbc, 1, S)

    stage[0:half] = q_ref[0].astype(F32)
    deinterleave(qc, 0, half)

    for g, d in enumerate(DIL_STRIDES):
        L = S // d
        n = half // d
        for r in range(d):
            for jj in range(n // tq):
                i0 = hf * n + jj * tq
                start = jnp.clip(i0 - BAND_HALF, 0, L - win)
                kv_rows = pl.ds(pl.multiple_of(r * L + start, BAND_HALF), win)
                if g == 0:
                    q = q_ref[0, jj * tq:(jj + 1) * tq, :]
                    kwin, va, vb = k_ref[0, kv_rows, :], vac[0, kv_rows, :], vbc[0, kv_rows, :]
                    rows = pl.ds(jj * tq, tq)
                else:
                    q = qc[g - 1, r * n + jj * tq:r * n + (jj + 1) * tq, :]
                    kwin, va, vb = kc[g - 1, kv_rows, :], vac[g, kv_rows, :], vbc[g, kv_rows, :]
                    rows = pl.ds(jj * tq * d + r, tq, stride=d)
                bias = bias_ref[(i0 - start) // BAND_HALF]
                accs, ms = _band_tile(q, kwin, va, vb, bias, lo)
                for hd in range(2):
                    acc[g, hd, rows, :] = accs[hd]
                    mx[g, hd, rows, :] = ms[hd]

    chunk = 2 * tq
    lo_c = lax.broadcasted_iota(jnp.int32, (chunk, LANES), 1) < HEAD_DIM
    n_pat = len(DIL_STRIDES)
    for c in range(half // chunk):
        rows = slice(c * chunk, (c + 1) * chunk)
        outs = []
        for hd in range(2):
            m = [mx[g, hd, rows, :] for g in range(n_pat)]
            m_all = functools.reduce(jnp.maximum, m)
            tot = sum(jnp.exp2(m[g] - m_all) * acc[g, hd, rows, :] for g in range(n_pat))
            outs.append(tot * (1.0 / pltpu.roll(tot, HEAD_DIM, axis=1)))
        o_ref[0, rows, :] = jnp.where(lo_c, outs[0], outs[1]).astype(BF)


def _dilated_attention(qd, kd, vd):
    B, S, W = qd.shape
    assert all((w // 2) // d == BAND_HALF for w, d in DIL_PATTERNS)
    assert DIL_STRIDES == (1, 4, 16) and S // DIL_STRIDES[-1] == BAND_WIN
    assert (S // 2) % (TQ_BAND * DIL_STRIDES[-1]) == 0
    half = S // 2
    bias = _band_bias()
    n_pat = len(DIL_STRIDES)
    return pl.pallas_call(
        functools.partial(_dil_body, S=S),
        grid=(B, W // LANES, 2),
        in_specs=[
            pl.BlockSpec((1, half, LANES), lambda b, p, h: (b, h, p)),
            pl.BlockSpec((1, S, LANES), lambda b, p, h: (b, 0, p)),
            pl.BlockSpec((1, S, LANES), lambda b, p, h: (b, 0, p)),
            _resident(bias.shape),
        ],
        out_specs=pl.BlockSpec((1, half, LANES), lambda b, p, h: (b, h, p)),
        out_shape=jax.ShapeDtypeStruct((B, S, W), BF),
        scratch_shapes=[
            pltpu.VMEM((S, LANES), F32),
            pltpu.VMEM((S, LANES), F32),
            pltpu.VMEM((n_pat - 1, S, LANES), BF),
            pltpu.VMEM((n_pat, S, LANES), BF),
            pltpu.VMEM((n_pat, S, LANES), BF),
            pltpu.VMEM((n_pat - 1, half, LANES), BF),
            pltpu.VMEM((n_pat, 2, half, LANES), F32),
            pltpu.VMEM((n_pat, 2, half, LANES), F32),
        ],
        compiler_params=_cparams(3),
        name="dilated_attn",
    )(qd, kd, vd, bias)


NA_ROWS_WIN = 10
NA_DR = 2 * NA_KH - 1
NA_DC = 2 * NA_KW - 1


def _bias2_body(rpb_ref, o_ref):
    h = pl.program_id(0)
    d = pl.program_id(1)
    cq = lax.broadcasted_iota(jnp.int32, (GRID_W, LANES), 0)
    lane = lax.broadcasted_iota(jnp.int32, (GRID_W, LANES), 1)
    kc = lane % GRID_W
    half = lane // GRID_W
    dc = kc - cq + (NA_KW - 1)
    c0 = jnp.clip(cq - NA_KW // 2, 0, GRID_W - NA_KW)
    dr = d - 1 + half
    ok = (kc >= c0) & (kc < c0 + NA_KW) & (dr >= 0) & (dr < NA_DR)
    val = jnp.zeros((GRID_W, LANES), F32)
    for hf in range(2):
        base = (h * NA_DR + jnp.clip(d - 1 + hf, 0, NA_DR - 1)) * NA_DC
        for j in range(NA_DC):
            val = jnp.where((dc == j) & (half == hf), rpb_ref[base + j], val)
    o_ref[0, 0] = jnp.where(ok, val * LOG2E, NEG_INF)


def _bias2(rpb_l):
    return pl.pallas_call(
        _bias2_body,
        grid=(H_NA, NA_DR + 1),
        in_specs=[pl.BlockSpec(memory_space=pltpu.SMEM)],
        out_specs=pl.BlockSpec((1, 1, GRID_W, LANES), lambda h, d: (h, d, 0, 0)),
        out_shape=jax.ShapeDtypeStruct((H_NA, NA_DR + 1, GRID_W, LANES), F32),
        compiler_params=_cparams(2),
        name="nbr_bias",
    )(rpb_l.reshape(-1))


NA_TILE_VARIANTS = 5
NA_TILES_PER_STEP = 4


def _nbr_window_row(j, rows):
    return jnp.clip(2 * j - NA_KH // 2, 0, rows - NA_ROWS_WIN)


def _tile_bias_body(b2_ref, o_ref, *, rows):
    v = pl.program_id(0)
    h = pl.program_id(1)
    n_tiles = rows // 2
    j = jnp.where(v < 2, v, jnp.where(v == 2, 2, v + (n_tiles - NA_TILE_VARIANTS)))
    rs = _nbr_window_row(j, rows)
    lane_b = lax.broadcasted_iota(jnp.int32, (GRID_W, LANES), 1) // GRID_W
    bias_rows = []
    for qi in range(2):
        rq = 2 * j + qi
        r0 = jnp.clip(rq - NA_KH // 2, 0, rows - NA_KH)
        blocks = []
        for kk in range(NA_ROWS_WIN // 2):
            kr = rs + 2 * kk
            d = jnp.clip(kr - rq + NA_KH, 0, NA_DR)
            krl = kr + lane_b
            ok = (krl >= r0) & (krl < r0 + NA_KH)
            blocks.append(jnp.where(ok, b2_ref[h, d], NEG_INF))
        bias_rows.append(jnp.concatenate(blocks, axis=1))
    o_ref[0, 0] = jnp.concatenate(bias_rows, axis=0)


def _tile_bias(bias2, rows):
    assert rows // 2 > NA_TILE_VARIANTS
    win = NA_ROWS_WIN * GRID_W
    return pl.pallas_call(
        functools.partial(_tile_bias_body, rows=rows),
        grid=(NA_TILE_VARIANTS, H_NA),
        in_specs=[_resident(bias2.shape)],
        out_specs=pl.BlockSpec((1, 1, TQ_BAND, win), lambda v, h: (v, h, 0, 0)),
        out_shape=jax.ShapeDtypeStruct((NA_TILE_VARIANTS, H_NA, TQ_BAND, win), F32),
        compiler_params=_cparams(2),
        name="nbr_tile_bias",
    )(bias2)


def _nbr_body(q_ref, k_ref, v_ref, bias_ref, o_ref, va, vb, *, rows):
    tq = TQ_BAND
    win = NA_ROWS_WIN * GRID_W
    n_tiles = rows // 2
    S = rows * GRID_W
    jt = pl.program_id(1)

    @pl.when(jt == 0)
    def _():
        lo_s = lax.broadcasted_iota(jnp.int32, (S, LANES), 1) < HEAD_DIM
        for hp in range(H_NA // 2):
            v = v_ref[0, :, hp * LANES:(hp + 1) * LANES]
            va[hp] = jnp.where(lo_s, v, jnp.ones_like(v))
            vb[hp] = jnp.where(lo_s, jnp.ones_like(v), v)

    lo = lax.broadcasted_iota(jnp.int32, (tq, LANES), 1) < HEAD_DIM
    for t in range(NA_TILES_PER_STEP):
        j = jt * NA_TILES_PER_STEP + t
        start = pl.multiple_of(_nbr_window_row(j, rows) * GRID_W, 2 * GRID_W)
        var = jnp.where(j < 2, j, jnp.where(j >= n_tiles - 2, j - (n_tiles - NA_TILE_VARIANTS), 2))
        for hp in range(H_NA // 2):
            cols = slice(hp * LANES, (hp + 1) * LANES)
            q = q_ref[0, t * tq:(t + 1) * tq, cols]
            zero = jnp.zeros_like(q)
            qs = jnp.concatenate([jnp.where(lo, q, zero), jnp.where(lo, zero, q)], axis=0)
            kw = k_ref[0, pl.ds(start, win), cols]
            bias = jnp.concatenate([bias_ref[var, 2 * hp], bias_ref[var, 2 * hp + 1]], axis=0)
            s = lax.dot_general(qs, kw, (((1,), (1,)), ((), ())),
                                preferred_element_type=F32) + bias
            m = jnp.max(s, axis=1, keepdims=True)
            p = jnp.exp2(s - m).astype(BF)
            outs = []
            for hh, vref in enumerate((va, vb)):
                a = jnp.dot(p[hh * tq:(hh + 1) * tq], vref[hp, pl.ds(start, win), :],
                            preferred_element_type=F32)
                outs.append(a * (1.0 / pltpu.roll(a, HEAD_DIM, axis=1)))
            o_ref[0, t * tq:(t + 1) * tq, cols] = jnp.where(lo, outs[0], outs[1]).astype(BF)


def _nbr_attention(qa, ka, va, bias2):
    B, S, W = qa.shape
    rows = S // GRID_W
    tq = TQ_BAND * NA_TILES_PER_STEP
    assert S % tq == 0
    bias_full = _tile_bias(bias2, rows)
    qblk = pl.BlockSpec((1, tq, W), lambda b, j: (b, j, 0))
    kblk = pl.BlockSpec((1, S, W), lambda b, j: (b, 0, 0))
    return pl.pallas_call(
        functools.partial(_nbr_body, rows=rows),
        grid=(B, S // tq),
        in_specs=[qblk, kblk, kblk, _resident(bias_full.shape)],
        out_specs=qblk,
        scratch_shapes=[pltpu.VMEM((H_NA // 2, S, LANES), BF) for _ in range(2)],
        out_shape=jax.ShapeDtypeStruct((B, S, W), BF),
        compiler_params=_cparams(2),
        name="nbr_attn",
    )(qa, ka, va, bias_full)


def _gqa_body(q_ref, kv_ref, o_ref, *, S):
    tq, tk = TQ_GQA, TK_GQA
    g = H_GQ // H_GKV
    lane = lax.broadcasted_iota(jnp.int32, (tq, LANES), 1)
    lo = lane < HEAD_DIM
    qs = []
    for kvh in range(H_GKV):
        mine = lo if kvh == 0 else jnp.logical_not(lo)
        parts = []
        for i in range(g):
            h = g * kvh + i
            chunk = q_ref[0, :, (h // 2) * LANES:(h // 2 + 1) * LANES]
            if (h % 2) != kvh:
                chunk = pltpu.roll(chunk.astype(F32), HEAD_DIM, axis=1).astype(BF)
            parts.append(jnp.where(mine, chunk, jnp.zeros_like(chunk)))
        qs.append(jnp.concatenate(parts, axis=0))

    def step(c, carry):
        rows = pl.ds(pl.multiple_of(c * tk, tk), tk)
        kblk = kv_ref[0, rows, 0:LANES]
        new = []
        for kvh in range(H_GKV):
            m, acc = carry[kvh]
            vblk = kv_ref[0, rows, (1 + kvh) * LANES:(2 + kvh) * LANES]
            s = lax.dot_general(qs[kvh], kblk, (((1,), (1,)), ((), ())),
                                preferred_element_type=F32)
            m_new = jnp.maximum(m, jnp.max(s, axis=1, keepdims=True))
            p = jnp.exp2(s - m_new)
            acc = jnp.exp2(m - m_new) * acc + jnp.dot(p.astype(BF), vblk,
                                                      preferred_element_type=F32)
            new.append((m_new, acc))
        return tuple(new)

    init = tuple((jnp.full((g * tq, 1), NEG_INF, F32), jnp.zeros((g * tq, LANES), F32))
                 for _ in range(H_GKV))
    fin = lax.fori_loop(0, S // tk, step, init, unroll=True)
    res = [acc * (1.0 / pltpu.roll(acc, HEAD_DIM, axis=1)) for _, acc in fin]
    for c in range(GQ_W // LANES):
        halves = []
        for hh in range(2):
            h = 2 * c + hh
            kvh, i = h // g, h % g
            v = res[kvh][i * tq:(i + 1) * tq]
            if kvh != hh:
                v = pltpu.roll(v, HEAD_DIM, axis=1)
            halves.append(v)
        o_ref[0, :, c * LANES:(c + 1) * LANES] = jnp.where(lo, halves[0], halves[1]).astype(BF)


def _gqa_attention(qc, kvc):
    B, S, W = qc.shape
    ones = jnp.ones((B, S, HEAD_DIM), BF)
    k2, v0, v1 = kvc[..., :GKV_W], kvc[..., GKV_W:GKV_W + HEAD_DIM], kvc[..., GKV_W + HEAD_DIM:]
    kvc = jnp.concatenate([k2, v0, ones, ones, v1], axis=-1)
    qblk = pl.BlockSpec((1, TQ_GQA, W), lambda b, j: (b, j, 0))
    kvblk = pl.BlockSpec((1, S, kvc.shape[-1]), lambda b, j: (b, 0, 0))
    return pl.pallas_call(
        functools.partial(_gqa_body, S=S),
        grid=(B, S // TQ_GQA),
        in_specs=[qblk, kvblk],
        out_specs=qblk,
        out_shape=jax.ShapeDtypeStruct((B, S, W), BF),
        compiler_params=_cparams(2),
        name="gqa_attn",
    )(qc, kvc)


def _outproj_body(oa_ref, od_ref, oc_ref, og_ref, x_ref, w_ref, y_ref):
    na = _rms(oa_ref[...].astype(F32), og_ref[:, :NA_W])
    nd = _rms(od_ref[...].astype(F32), og_ref[:, NA_W:NA_W + DIL_W])
    nc = _rms(oc_ref[...].astype(F32), og_ref[:, NA_W + DIL_W:])
    mix = jnp.concatenate([na, nd, nc], axis=1).astype(BF)
    y_ref[...] = x_ref[...] + jnp.dot(mix, w_ref[...], preferred_element_type=F32)


def _outproj(oa, od, oc, og, x2, w_out):
    N, D = x2.shape
    tm = TM_PROJ
    row = lambda i: (i, 0)
    return pl.pallas_call(
        _outproj_body,
        grid=(N // tm,),
        in_specs=[
            pl.BlockSpec((tm, NA_W), row), pl.BlockSpec((tm, DIL_W), row),
            pl.BlockSpec((tm, GQ_W), row), _resident(og.shape),
            pl.BlockSpec((tm, D), row), _resident(w_out.shape),
        ],
        out_specs=pl.BlockSpec((tm, D), row),
        out_shape=jax.ShapeDtypeStruct((N, D), F32),
        compiler_params=_cparams(1),
        name="outproj",
    )(oa, od, oc, og, x2, w_out)


def _ffn_body(y_ref, yp_ref, yn_ref, g2_ref, wgu_ref, cw_ref, cb_ref, wd_ref, o_ref,
              hext, gs, act, *, tiles_per_seq, d_ff):
    tm, halo = TM_FFN, FFN_HALO
    i = pl.program_id(0) % tiles_per_seq
    g2 = g2_ref[...]
    y = y_ref[...]
    prev_ok = (i > 0).astype(F32)
    next_ok = (i < tiles_per_seq - 1).astype(F32)
    hext[0:halo] = (_rms(yp_ref[...], g2) * prev_ok).astype(BF)
    hext[halo:halo + tm] = _rms(y, g2).astype(BF)
    hext[halo + tm:] = (_rms(yn_ref[...], g2) * next_ok).astype(BF)
    for c in range(d_ff // MXU_N):
        cols = slice(c * MXU_N, (c + 1) * MXU_N)
        ucols = slice(d_ff + c * MXU_N, d_ff + (c + 1) * MXU_N)
        gs[...] = jnp.dot(hext[...], wgu_ref[:, cols], preferred_element_type=F32)
        u = jnp.dot(hext[halo:halo + tm], wgu_ref[:, ucols], preferred_element_type=F32)
        gc = gs[halo - 1:halo - 1 + tm] * cw_ref[0:1, cols] + cb_ref[:, cols]
        gc = gc + gs[halo:halo + tm] * cw_ref[1:2, cols]
        gc = gc + gs[halo + 1:halo + 1 + tm] * cw_ref[2:3, cols]
        gelu = 0.5 * gc * (1.0 + lax.erf(gc * (0.5 ** 0.5)))
        act[:, cols] = (gelu * u).astype(BF)
    o_ref[...] = y + jnp.dot(act[...], wd_ref[...], preferred_element_type=F32)


def _ffn(y2, g2, wgu, cw, cb, wd, S):
    N, D = y2.shape
    d_ff = wd.shape[0]
    tm, halo = TM_FFN, FFN_HALO
    assert S % tm == 0 and d_ff % MXU_N == 0
    r = tm // halo
    nblk = N // halo
    row = lambda i: (i, 0)
    return pl.pallas_call(
        functools.partial(_ffn_body, tiles_per_seq=S // tm, d_ff=d_ff),
        grid=(N // tm,),
        in_specs=[
            pl.BlockSpec((tm, D), row),
            pl.BlockSpec((halo, D), lambda i: (jnp.maximum(i * r - 1, 0), 0)),
            pl.BlockSpec((halo, D), lambda i: (jnp.minimum((i + 1) * r, nblk - 1), 0)),
            _resident(g2.shape), _resident(wgu.shape), _resident(cw.shape),
            _resident(cb.shape), _resident(wd.shape),
        ],
        out_specs=pl.BlockSpec((tm, D), row),
        out_shape=jax.ShapeDtypeStruct((N, D), F32),
        scratch_shapes=[
            pltpu.VMEM((tm + 2 * halo, D), BF),
            pltpu.VMEM((tm + 2 * halo, MXU_N), F32),
            pltpu.VMEM((tm, d_ff), BF),
        ],
        compiler_params=_cparams(1),
        name="ffn",
    )(y2, y2, y2, g2, wgu, cw, cb, wd)


def _rope_cos_sin(pos, dim, theta):
    inv = theta ** (-jnp.arange(0, dim, 2, dtype=F32) / dim)
    ang = pos.astype(F32)[:, None] * inv[None, :]
    return jnp.cos(ang), jnp.sin(ang)


def _rotary_tables(S):
    t = jnp.arange(S, dtype=jnp.int32)
    cos1, sin1 = _rope_cos_sin(t, ROPE_DIMS, ROPE_THETA)
    rest = HEAD_DIM - ROPE_DIMS
    cb = jnp.concatenate([cos1, cos1, jnp.ones((S, rest), F32)], axis=1)
    sb = jnp.concatenate([-sin1, sin1, jnp.zeros((S, rest), F32)], axis=1)
    cr, sr = _rope_cos_sin(t // GRID_W, HEAD_DIM // 2, AXIAL_THETA)
    cc, sc = _rope_cos_sin(t % GRID_W, HEAD_DIM // 2, AXIAL_THETA)
    ca = jnp.concatenate([cr, cr, cc, cc], axis=1)
    sa = jnp.concatenate([-sr, sr, -sc, sc], axis=1)
    two = lambda a: jnp.concatenate([a, a], axis=1)
    return two(cb), two(sb), two(ca), two(sa)


def _permute_in_columns(w):
    sizes = (NA_W, NA_W, NA_W, DIL_W, DIL_W, DIL_W, GQ_W, GKV_W, GKV_W)
    offs = [0]
    for s in sizes:
        offs.append(offs[-1] + s)
    seg = [w[:, offs[i]:offs[i + 1]] for i in range(len(sizes))]
    qa, ka, va, qd, kd, vd, qc, kc, vc = seg
    return jnp.concatenate([qa, ka, qd, kd, qc, kc, va, vd, vc], axis=1)


def _gain_vector(qg, kg):
    scale = HEAD_DIM ** -0.5
    parts = [
        jnp.tile(qg[0] * (scale * LOG2E), H_NA), jnp.tile(kg[0], H_NA),
        jnp.tile(qg[1] * (scale * LOG2E), H_DIL), jnp.tile(kg[1], H_DIL),
        jnp.tile(qg[2] * (scale * LOG2E), H_GQ), jnp.tile(kg[2], H_GKV),
    ]
    return jnp.concatenate(parts)[None, :]


def kernel(x, norm1_g, w_in, q_norm_g, k_norm_g, rpb, out_norm_g, w_out, norm2_g,
           w_gate_up, conv_w, conv_b, w_down):
    B, S, D = x.shape
    depth = w_in.shape[0]
    N = B * S
    assert S == GRID_W * GRID_W and S % TM_PROJ == 0 and S % TQ_GQA == 0 and S % TK_GQA == 0
    tabs = _rotary_tables(S)
    idx = jnp.arange(MXU_N) // HEAD_DIM
    gsum = (idx[:, None] == idx[None, :]).astype(BF)
    x2 = x.reshape(N, D)
    for l in range(depth):
        w_perm = _permute_in_columns(w_in[l].astype(BF))
        gain = _gain_vector(q_norm_g[l], k_norm_g[l])
        qa, ka, va, qd, kd, vd, qc, kvc = _inproj(
            x2, norm1_g[l][None, :], w_perm, gsum, gain, tabs, S)
        seq = lambda a: a.reshape(B, S, a.shape[-1])
        out_a = _nbr_attention(seq(qa), seq(ka), seq(va), _bias2(rpb[l]))
        out_d = _dilated_attention(seq(qd), seq(kd), seq(vd))
        out_c = _gqa_attention(seq(qc), seq(kvc))
        y2 = _outproj(out_a.reshape(N, NA_W), out_d.reshape(N, DIL_W), out_c.reshape(N, GQ_W),
                      out_norm_g[l][None, :], x2, w_out[l].astype(BF))
        x2 = _ffn(y2, norm2_g[l][None, :], w_gate_up[l].astype(BF), conv_w[l],
                  conv_b[l][None, :], w_down[l].astype(BF), S)
    return x2.reshape(B, S, D)
```

```python
import functools

import jax
import jax.numpy as jnp
from jax import lax
from jax.experimental import pallas as pl
from jax.experimental.pallas import tpu as pltpu

BF = jnp.bfloat16
F32 = jnp.float32

HEAD_DIM = 64
H_NA, H_DIL, H_GQ, H_GKV = 4, 6, 6, 2
NA_W, DIL_W, GQ_W, GKV_W = 256, 384, 384, 128
GRID_W = 64
NA_KH = 8
NA_KW = 16
DIL_PATTERNS = ((128, 1), (512, 4), (2048, 16))
BAND_HALF = 64
ROPE_THETA = 500000.0
ROPE_DIMS = 16
AXIAL_THETA = 10000.0
CONV_W = 3
EPS = 1e-6
NEG_INF = -1e30
LOG2E = 1.4426950408889634

LANES = 128
MXU_N = 256
VMEM_LIMIT = 56 * 1024 * 1024

TM_PROJ = 512
TM_IN = 1024
PROJ_SUB = 256
TM_FFN = 512
FFN_HALO = 16
TQ_BAND = 128
BAND_WIN = 256
TQ_GQA = 256
TK_GQA = 512


def _cparams(n_axes):
    return pltpu.CompilerParams(
        dimension_semantics=("arbitrary",) * n_axes, vmem_limit_bytes=VMEM_LIMIT)


def _rms(v, g):
    ms = jnp.mean(v * v, axis=-1, keepdims=True)
    return v * lax.rsqrt(ms + EPS) * g


def _resident(shape):
    nd = len(shape)
    return pl.BlockSpec(shape, lambda *_: (0,) * nd, pipeline_mode=pl.Buffered(1))


def _rotary(p, cos, sin, first, half):
    up = pltpu.roll(p, LANES - half, axis=1)
    dn = pltpu.roll(p, half, axis=1)
    return p * cos + jnp.where(first, up, dn) * sin


def _inproj_body(x_ref, g1_ref, w_ref, gs_ref, gain_ref, cb_ref, sb_ref, cc_ref, sc_ref,
                 qa_ref, ka_ref, va_ref, qd_ref, kd_ref, vd_ref, qc_ref, kvc_ref):
    lane = lax.broadcasted_iota(jnp.int32, (1, LANES), 1)
    first_b = (lane % HEAD_DIM) < (ROPE_DIMS // 2)
    first_c = (lane % (HEAD_DIM // 2)) < (HEAD_DIM // 4)
    lo_half = lane < HEAD_DIM
    dests = (
        ((qa_ref, 0), (qa_ref, 128)),
        ((ka_ref, 0), (ka_ref, 128)),
        ((qd_ref, 0), (qd_ref, 128)),
        ((qd_ref, 256), (kd_ref, 0)),
        ((kd_ref, 128), (kd_ref, 256)),
        ((qc_ref, 0), (qc_ref, 128)),
        ((qc_ref, 256), (kvc_ref, 0)),
        ((va_ref, 0), (va_ref, 128)),
        ((vd_ref, 0), (vd_ref, 128)),
        ((vd_ref, 256), (kvc_ref, 128)),
    )
    for r0 in range(0, x_ref.shape[0], PROJ_SUB):
        rows = slice(r0, r0 + PROJ_SUB)
        h = _rms(x_ref[rows, :], g1_ref[...]).astype(BF)
        n_norm = gain_ref.shape[1]
        pq = jnp.dot(h, w_ref[:, :n_norm], preferred_element_type=F32)
        pv = jnp.dot(h, w_ref[:, n_norm:], preferred_element_type=F32)
        sq = (pq * pq).astype(BF)
        for c in range(10):
            cols = slice(c * MXU_N, (c + 1) * MXU_N)
            if c < 7:
                ss = jnp.dot(sq[:, cols], gs_ref[...], preferred_element_type=F32)
                p = pq[:, cols] * lax.rsqrt(ss * (1.0 / HEAD_DIM) + EPS) * gain_ref[:, cols]
            else:
                p = pv[:, (c - 7) * MXU_N:(c - 6) * MXU_N]
            halves = [p[:, :LANES], p[:, LANES:]]
            if 2 <= c <= 4:
                halves = [_rotary(v, cb_ref[rows, :], sb_ref[rows, :], first_b, ROPE_DIMS // 2)
                          for v in halves]
            elif 5 <= c <= 6:
                halves = [_rotary(v, cc_ref[rows, :], sc_ref[rows, :], first_c, HEAD_DIM // 4)
                          for v in halves]
            for v, (ref, off) in zip(halves, dests[c]):
                if ref is kvc_ref and off == LANES:
                    ref[rows, LANES:2 * LANES] = jnp.where(lo_half, v, 1.0).astype(BF)
                    ref[rows, 2 * LANES:3 * LANES] = jnp.where(lo_half, 1.0, v).astype(BF)
                else:
                    ref[rows, off:off + LANES] = v.astype(BF)


def _inproj(x2, g1, w_perm, gsum, gain, tabs, S):
    N, D = x2.shape
    tm = TM_IN
    assert S % tm == 0 and tm % PROJ_SUB == 0
    nt_seq = S // tm
    row = lambda i: (i, 0)
    tab = pl.BlockSpec((tm, LANES), lambda i: (i % nt_seq, 0))
    widths = (NA_W, NA_W, NA_W, DIL_W, DIL_W, DIL_W, GQ_W, 3 * GKV_W)
    return pl.pallas_call(
        _inproj_body,
        grid=(N // tm,),
        in_specs=[
            pl.BlockSpec((tm, D), row),
            _resident(g1.shape),
            _resident(w_perm.shape),
            _resident(gsum.shape),
            _resident(gain.shape),
            tab, tab, tab, tab,
        ],
        out_specs=[pl.BlockSpec((tm, w), row) for w in widths],
        out_shape=[jax.ShapeDtypeStruct((N, w), BF) for w in widths],
        compiler_params=_cparams(1),
        name="inproj",
    )(x2, g1, w_perm, gsum, gain, *tabs)


DIL_STRIDES = tuple(d for _, d in DIL_PATTERNS)


def _band_bias():
    r = jnp.arange(TQ_BAND)[:, None]
    c = jnp.arange(BAND_WIN)[None, :]
    masks = [jnp.where(jnp.abs(r + off - c) <= BAND_HALF, 0.0, NEG_INF).astype(F32)
             for off in (0, BAND_HALF, 2 * BAND_HALF)]
    return jnp.stack([jnp.concatenate([m, m], axis=0) for m in masks])


def _band_tile(q, kwin, vwin_a, vwin_b, bias, lo):
    tq = TQ_BAND
    zero = jnp.zeros_like(q)
    qs = jnp.concatenate([jnp.where(lo, q, zero), jnp.where(lo, zero, q)], axis=0)
    s = lax.dot_general(qs, kwin, (((1,), (1,)), ((), ())), preferred_element_type=F32) + bias
    m = jnp.max(s, axis=1, keepdims=True)
    p = jnp.exp2(s - m).astype(BF)
    acc_a = jnp.dot(p[:tq], vwin_a, preferred_element_type=F32)
    acc_b = jnp.dot(p[tq:], vwin_b, preferred_element_type=F32)
    m_rep = jnp.broadcast_to(m, (2 * tq, LANES))
    return (acc_a, acc_b), (m_rep[:tq], m_rep[tq:])


def _dil_body(q_ref, k_ref, v_ref, bias_ref, o_ref,
              stage, stage4, kc, vac, vbc, qc, acc, mx, *, S):
    tq, win = TQ_BAND, BAND_WIN
    half = S // 2
    hf = pl.program_id(2)
    lo = lax.broadcasted_iota(jnp.int32, (tq, LANES), 1) < HEAD_DIM

    def deinterleave(dst, g0, rows):
        n4, n16 = rows // 4, rows // 16
        for r in range(4):
            cls = stage[pl.ds(r, n4, stride=4), :]
            stage4[r * n4:(r + 1) * n4, :] = cls
            dst[g0, r * n4:(r + 1) * n4, :] = cls.astype(BF)
        for r in range(16):
            cls = stage4[pl.ds((r % 4) * n4 + r // 4, n16, stride=4), :]
            dst[g0 + 1, r * n16:(r + 1) * n16, :] = cls.astype(BF)

    @pl.when(hf == 0)
    def _():
        lo_s = lax.broadcasted_iota(jnp.int32, (S, LANES), 1) < HEAD_DIM
        stage[...] = k_ref[0].astype(F32)
        deinterleave(kc, 0, S)
        v = v_ref[0].astype(F32)
        stage[...] = jnp.where(lo_s, v, 1.0)
        vac[0] = stage[...].astype(BF)
        deinterleave(vac, 1, S)
        stage[...] = jnp.where(lo_s, 1.0, v)
        vbc[0] = stage[...].astype(BF)
        deinterleave(vbc, 1, S)

    stage[0:half] = q_ref[0].astype(F32)
    deinterleave(qc, 0, half)

    for g, d in enumerate(DIL_STRIDES):
        L = S // d
        n = half // d
        for r in range(d):
            for jj in range(n // tq):
                i0 = hf * n + jj * tq
                start = jnp.clip(i0 - BAND_HALF, 0, L - win)
                kv_rows = pl.ds(pl.multiple_of(r * L + start, BAND_HALF), win)
                if g == 0:
                    q = q_ref[0, jj * tq:(jj + 1) * tq, :]
                    kwin, va, vb = k_ref[0, kv_rows, :], vac[0, kv_rows, :], vbc[0, kv_rows, :]
                    rows = pl.ds(jj * tq, tq)
                else:
                    q = qc[g - 1, r * n + jj * tq:r * n + (jj + 1) * tq, :]
                    kwin, va, vb = kc[g - 1, kv_rows, :], vac[g, kv_rows, :], vbc[g, kv_rows, :]
                    rows = pl.ds(jj * tq * d + r, tq, stride=d)
                bias = bias_ref[(i0 - start) // BAND_HALF]
                accs, ms = _band_tile(q, kwin, va, vb, bias, lo)
                for hd in range(2):
                    acc[g, hd, rows, :] = accs[hd]
                    mx[g, hd, rows, :] = ms[hd]

    chunk = 2 * tq
    lo_c = lax.broadcasted_iota(jnp.int32, (chunk, LANES), 1) < HEAD_DIM
    n_pat = len(DIL_STRIDES)
    for c in range(half // chunk):
        rows = slice(c * chunk, (c + 1) * chunk)
        outs = []
        for hd in range(2):
            m = [mx[g, hd, rows, :] for g in range(n_pat)]
            m_all = functools.reduce(jnp.maximum, m)
            tot = sum(jnp.exp2(m[g] - m_all) * acc[g, hd, rows, :] for g in range(n_pat))
            outs.append(tot * (1.0 / pltpu.roll(tot, HEAD_DIM, axis=1)))
        o_ref[0, rows, :] = jnp.where(lo_c, outs[0], outs[1]).astype(BF)


def _dilated_attention(qd, kd, vd):
    B, S, W = qd.shape
    assert all((w // 2) // d == BAND_HALF for w, d in DIL_PATTERNS)
    assert DIL_STRIDES == (1, 4, 16) and S // DIL_STRIDES[-1] == BAND_WIN
    assert (S // 2) % (TQ_BAND * DIL_STRIDES[-1]) == 0
    half = S // 2
    bias = _band_bias()
    n_pat = len(DIL_STRIDES)
    return pl.pallas_call(
        functools.partial(_dil_body, S=S),
        grid=(B, W // LANES, 2),
        in_specs=[
            pl.BlockSpec((1, half, LANES), lambda b, p, h: (b, h, p)),
            pl.BlockSpec((1, S, LANES), lambda b, p, h: (b, 0, p)),
            pl.BlockSpec((1, S, LANES), lambda b, p, h: (b, 0, p)),
            _resident(bias.shape),
        ],
        out_specs=pl.BlockSpec((1, half, LANES), lambda b, p, h: (b, h, p)),
        out_shape=jax.ShapeDtypeStruct((B, S, W), BF),
        scratch_shapes=[
            pltpu.VMEM((S, LANES), F32),
            pltpu.VMEM((S, LANES), F32),
            pltpu.VMEM((n_pat - 1, S, LANES), BF),
            pltpu.VMEM((n_pat, S, LANES), BF),
            pltpu.VMEM((n_pat, S, LANES), BF),
            pltpu.VMEM((n_pat - 1, half, LANES), BF),
            pltpu.VMEM((n_pat, 2, half, LANES), F32),
            pltpu.VMEM((n_pat, 2, half, LANES), F32),
        ],
        compiler_params=_cparams(3),
        name="dilated_attn",
    )(qd, kd, vd, bias)


NA_ROWS_WIN = 10
NA_DR = 2 * NA_KH - 1
NA_DC = 2 * NA_KW - 1


def _bias2_body(rpb_ref, o_ref):
    h = pl.program_id(0)
    d = pl.program_id(1)
    cq = lax.broadcasted_iota(jnp.int32, (GRID_W, LANES), 0)
    lane = lax.broadcasted_iota(jnp.int32, (GRID_W, LANES), 1)
    kc = lane % GRID_W
    half = lane // GRID_W
    dc = kc - cq + (NA_KW - 1)
    c0 = jnp.clip(cq - NA_KW // 2, 0, GRID_W - NA_KW)
    dr = d - 1 + half
    ok = (kc >= c0) & (kc < c0 + NA_KW) & (dr >= 0) & (dr < NA_DR)
    val = jnp.zeros((GRID_W, LANES), F32)
    for hf in range(2):
        base = (h * NA_DR + jnp.clip(d - 1 + hf, 0, NA_DR - 1)) * NA_DC
        for j in range(NA_DC):
            val = jnp.where((dc == j) & (half == hf), rpb_ref[base + j], val)
    o_ref[0, 0] = jnp.where(ok, val * LOG2E, NEG_INF)


def _bias2(rpb_l):
    return pl.pallas_call(
        _bias2_body,
        grid=(H_NA, NA_DR + 1),
        in_specs=[pl.BlockSpec(memory_space=pltpu.SMEM)],
        out_specs=pl.BlockSpec((1, 1, GRID_W, LANES), lambda h, d: (h, d, 0, 0)),
        out_shape=jax.ShapeDtypeStruct((H_NA, NA_DR + 1, GRID_W, LANES), F32),
        compiler_params=_cparams(2),
        name="nbr_bias",
    )(rpb_l.reshape(-1))


NA_TILE_VARIANTS = 5
NA_TILES_PER_STEP = 4


def _nbr_window_row(j, rows):
    return jnp.clip(2 * j - NA_KH // 2, 0, rows - NA_ROWS_WIN)


def _tile_bias_body(b2_ref, o_ref, *, rows):
    v = pl.program_id(0)
    h = pl.program_id(1)
    n_tiles = rows // 2
    j = jnp.where(v < 2, v, jnp.where(v == 2, 2, v + (n_tiles - NA_TILE_VARIANTS)))
    rs = _nbr_window_row(j, rows)
    lane_b = lax.broadcasted_iota(jnp.int32, (GRID_W, LANES), 1) // GRID_W
    bias_rows = []
    for qi in range(2):
        rq = 2 * j + qi
        r0 = jnp.clip(rq - NA_KH // 2, 0, rows - NA_KH)
        blocks = []
        for kk in range(NA_ROWS_WIN // 2):
            kr = rs + 2 * kk
            d = jnp.clip(kr - rq + NA_KH, 0, NA_DR)
            krl = kr + lane_b
            ok = (krl >= r0) & (krl < r0 + NA_KH)
            blocks.append(jnp.where(ok, b2_ref[h, d], NEG_INF))
        bias_rows.append(jnp.concatenate(blocks, axis=1))
    o_ref[0, 0] = jnp.concatenate(bias_rows, axis=0)


def _tile_bias(bias2, rows):
    assert rows // 2 > NA_TILE_VARIANTS
    win = NA_ROWS_WIN * GRID_W
    return pl.pallas_call(
        functools.partial(_tile_bias_body, rows=rows),
        grid=(NA_TILE_VARIANTS, H_NA),
        in_specs=[_resident(bias2.shape)],
        out_specs=pl.BlockSpec((1, 1, TQ_BAND, win), lambda v, h: (v, h, 0, 0)),
        out_shape=jax.ShapeDtypeStruct((NA_TILE_VARIANTS, H_NA, TQ_BAND, win), F32),
        compiler_params=_cparams(2),
        name="nbr_tile_bias",
    )(bias2)


def _nbr_body(q_ref, k_ref, v_ref, bias_ref, o_ref, va, vb, *, rows):
    tq = TQ_BAND
    win = NA_ROWS_WIN * GRID_W
    n_tiles = rows // 2
    S = rows * GRID_W
    jt = pl.program_id(1)

    @pl.when(jt == 0)
    def _():
        lo_s = lax.broadcasted_iota(jnp.int32, (S, LANES), 1) < HEAD_DIM
        for hp in range(H_NA // 2):
            v = v_ref[0, :, hp * LANES:(hp + 1) * LANES]
            va[hp] = jnp.where(lo_s, v, jnp.ones_like(v))
            vb[hp] = jnp.where(lo_s, jnp.ones_like(v), v)

    lo = lax.broadcasted_iota(jnp.int32, (tq, LANES), 1) < HEAD_DIM
    for t in range(NA_TILES_PER_STEP):
        j = jt * NA_TILES_PER_STEP + t
        start = pl.multiple_of(_nbr_window_row(j, rows) * GRID_W, 2 * GRID_W)
        var = jnp.where(j < 2, j, jnp.where(j >= n_tiles - 2, j - (n_tiles - NA_TILE_VARIANTS), 2))
        for hp in range(H_NA // 2):
            cols = slice(hp * LANES, (hp + 1) * LANES)
            q = q_ref[0, t * tq:(t + 1) * tq, cols]
            zero = jnp.zeros_like(q)
            qs = jnp.concatenate([jnp.where(lo, q, zero), jnp.where(lo, zero, q)], axis=0)
            kw = k_ref[0, pl.ds(start, win), cols]
            bias = jnp.concatenate([bias_ref[var, 2 * hp], bias_ref[var, 2 * hp + 1]], axis=0)
            s = lax.dot_general(qs, kw, (((1,), (1,)), ((), ())),
                                preferred_element_type=F32) + bias
            m = jnp.max(s, axis=1, keepdims=True)
            p = jnp.exp2(s - m).astype(BF)
            outs = []
            for hh, vref in enumerate((va, vb)):
                a = jnp.dot(p[hh * tq:(hh + 1) * tq], vref[hp, pl.ds(start, win), :],
                            preferred_element_type=F32)
                outs.append(a * (1.0 / pltpu.roll(a, HEAD_DIM, axis=1)))
            o_ref[0, t * tq:(t + 1) * tq, cols] = jnp.where(lo, outs[0], outs[1]).astype(BF)


def _nbr_attention(qa, ka, va, bias2):
    B, S, W = qa.shape
    rows = S // GRID_W
    tq = TQ_BAND * NA_TILES_PER_STEP
    assert S % tq == 0
    bias_full = _tile_bias(bias2, rows)
    qblk = pl.BlockSpec((1, tq, W), lambda b, j: (b, j, 0))
    kblk = pl.BlockSpec((1, S, W), lambda b, j: (b, 0, 0))
    return pl.pallas_call(
        functools.partial(_nbr_body, rows=rows),
        grid=(B, S // tq),
        in_specs=[qblk, kblk, kblk, _resident(bias_full.shape)],
        out_specs=qblk,
        scratch_shapes=[pltpu.VMEM((H_NA // 2, S, LANES), BF) for _ in range(2)],
        out_shape=jax.ShapeDtypeStruct((B, S, W), BF),
        compiler_params=_cparams(2),
        name="nbr_attn",
    )(qa, ka, va, bias_full)


def _gqa_body(q_ref, kv_ref, o_ref, *, S):
    tq, tk = TQ_GQA, TK_GQA
    g = H_GQ // H_GKV
    lane = lax.broadcasted_iota(jnp.int32, (tq, LANES), 1)
    lo = lane < HEAD_DIM
    qs = []
    for kvh in range(H_GKV):
        mine = lo if kvh == 0 else jnp.logical_not(lo)
        parts = []
        for i in range(g):
            h = g * kvh + i
            chunk = q_ref[0, :, (h // 2) * LANES:(h // 2 + 1) * LANES]
            if (h % 2) != kvh:
                chunk = pltpu.roll(chunk.astype(F32), HEAD_DIM, axis=1).astype(BF)
            parts.append(jnp.where(mine, chunk, jnp.zeros_like(chunk)))
        qs.append(jnp.concatenate(parts, axis=0))

    def step(c, carry):
        rows = pl.ds(pl.multiple_of(c * tk, tk), tk)
        kblk = kv_ref[0, rows, 0:LANES]
        new = []
        for kvh in range(H_GKV):
            m, acc = carry[kvh]
            vblk = kv_ref[0, rows, (1 + kvh) * LANES:(2 + kvh) * LANES]
            s = lax.dot_general(qs[kvh], kblk, (((1,), (1,)), ((), ())),
                                preferred_element_type=F32)
            m_new = jnp.maximum(m, jnp.max(s, axis=1, keepdims=True))
            p = jnp.exp2(s - m_new)
            acc = jnp.exp2(m - m_new) * acc + jnp.dot(p.astype(BF), vblk,
                                                      preferred_element_type=F32)
            new.append((m_new, acc))
        return tuple(new)

    init = tuple((jnp.full((g * tq, 1), NEG_INF, F32), jnp.zeros((g * tq, LANES), F32))
                 for _ in range(H_GKV))
    fin = lax.fori_loop(0, S // tk, step, init, unroll=True)
    res = [acc * (1.0 / pltpu.roll(acc, HEAD_DIM, axis=1)) for _, acc in fin]
    for c in range(GQ_W // LANES):
        halves = []
        for hh in range(2):
            h = 2 * c + hh
            kvh, i = h // g, h % g
            v = res[kvh][i * tq:(i + 1) * tq]
            if kvh != hh:
                v = pltpu.roll(v, HEAD_DIM, axis=1)
            halves.append(v)
        o_ref[0, :, c * LANES:(c + 1) * LANES] = jnp.where(lo, halves[0], halves[1]).astype(BF)


def _gqa_attention(qc, kvc):
    B, S, W = qc.shape
    qblk = pl.BlockSpec((1, TQ_GQA, W), lambda b, j: (b, j, 0))
    kvblk = pl.BlockSpec((1, S, kvc.shape[-1]), lambda b, j: (b, 0, 0))
    return pl.pallas_call(
        functools.partial(_gqa_body, S=S),
        grid=(B, S // TQ_GQA),
        in_specs=[qblk, kvblk],
        out_specs=qblk,
        out_shape=jax.ShapeDtypeStruct((B, S, W), BF),
        compiler_params=_cparams(2),
        name="gqa_attn",
    )(qc, kvc)


def _ffn_body(*refs, tiles_per_seq, d_ff):
    (oa, oa_p, oa_n, od, od_p, od_n, oc, oc_p, oc_n, x, x_p, x_n,
     og_ref, wo_ref, g2_ref, wgu_ref, cw_ref, cb_ref, wd_ref, o_ref, hext, gs, act) = refs
    tm, halo = TM_FFN, FFN_HALO
    i = pl.program_id(0) % tiles_per_seq

    def ext(main, prev, nxt):
        return jnp.concatenate([prev[...], main[...], nxt[...]], axis=0)

    na = _rms(ext(oa, oa_p, oa_n).astype(F32), og_ref[:, :NA_W])
    nd = _rms(ext(od, od_p, od_n).astype(F32), og_ref[:, NA_W:NA_W + DIL_W])
    nc = _rms(ext(oc, oc_p, oc_n).astype(F32), og_ref[:, NA_W + DIL_W:])
    mix = jnp.concatenate([na, nd, nc], axis=1).astype(BF)
    y_ext = ext(x, x_p, x_n) + jnp.dot(mix, wo_ref[...], preferred_element_type=F32)
    y = y_ext[halo:halo + tm]
    row = lax.broadcasted_iota(jnp.int32, (tm + 2 * halo, 1), 0)
    keep = ((row >= halo) | (i > 0)) & ((row < halo + tm) | (i < tiles_per_seq - 1))
    hext[...] = jnp.where(keep, _rms(y_ext, g2_ref[...]), 0.0).astype(BF)
    for c in range(d_ff // MXU_N):
        cols = slice(c * MXU_N, (c + 1) * MXU_N)
        ucols = slice(d_ff + c * MXU_N, d_ff + (c + 1) * MXU_N)
        gs[...] = jnp.dot(hext[...], wgu_ref[:, cols], preferred_element_type=F32)
        u = jnp.dot(hext[halo:halo + tm], wgu_ref[:, ucols], preferred_element_type=F32)
        gc = gs[halo - 1:halo - 1 + tm] * cw_ref[0:1, cols] + cb_ref[:, cols]
        gc = gc + gs[halo:halo + tm] * cw_ref[1:2, cols]
        gc = gc + gs[halo + 1:halo + 1 + tm] * cw_ref[2:3, cols]
        gelu = 0.5 * gc * (1.0 + lax.erf(gc * (0.5 ** 0.5)))
        act[:, cols] = (gelu * u).astype(BF)
    o_ref[...] = y + jnp.dot(act[...], wd_ref[...], preferred_element_type=F32)


def _ffn(oa, od, oc, x2, og, w_out, g2, wgu, cw, cb, wd, S):
    N, D = x2.shape
    d_ff = wd.shape[0]
    tm, halo = TM_FFN, FFN_HALO
    assert S % tm == 0 and d_ff % MXU_N == 0
    r = tm // halo
    nblk = N // halo
    row = lambda i: (i, 0)
    prev = lambda i: (jnp.maximum(i * r - 1, 0), 0)
    nxt = lambda i: (jnp.minimum((i + 1) * r, nblk - 1), 0)
    tiled, args = [], []
    for a in (oa, od, oc, x2):
        w = a.shape[1]
        tiled += [pl.BlockSpec((tm, w), row), pl.BlockSpec((halo, w), prev),
                  pl.BlockSpec((halo, w), nxt)]
        args += [a, a, a]
    params = (og, w_out, g2, wgu, cw, cb, wd)
    return pl.pallas_call(
        functools.partial(_ffn_body, tiles_per_seq=S // tm, d_ff=d_ff),
        grid=(N // tm,),
        in_specs=tiled + [_resident(p.shape) for p in params],
        out_specs=pl.BlockSpec((tm, D), row),
        out_shape=jax.ShapeDtypeStruct((N, D), F32),
        scratch_shapes=[
            pltpu.VMEM((tm + 2 * halo, D), BF),
            pltpu.VMEM((tm + 2 * halo, MXU_N), F32),
            pltpu.VMEM((tm, d_ff), BF),
        ],
        compiler_params=_cparams(1),
        name="ffn",
    )(*args, *params)


def _rope_cos_sin(pos, dim, theta):
    inv = theta ** (-jnp.arange(0, dim, 2, dtype=F32) / dim)
    ang = pos.astype(F32)[:, None] * inv[None, :]
    return jnp.cos(ang), jnp.sin(ang)


def _rotary_tables(S):
    t = jnp.arange(S, dtype=jnp.int32)
    cos1, sin1 = _rope_cos_sin(t, ROPE_DIMS, ROPE_THETA)
    rest = HEAD_DIM - ROPE_DIMS
    cb = jnp.concatenate([cos1, cos1, jnp.ones((S, rest), F32)], axis=1)
    sb = jnp.concatenate([-sin1, sin1, jnp.zeros((S, rest), F32)], axis=1)
    cr, sr = _rope_cos_sin(t // GRID_W, HEAD_DIM // 2, AXIAL_THETA)
    cc, sc = _rope_cos_sin(t % GRID_W, HEAD_DIM // 2, AXIAL_THETA)
    ca = jnp.concatenate([cr, cr, cc, cc], axis=1)
    sa = jnp.concatenate([-sr, sr, -sc, sc], axis=1)
    two = lambda a: jnp.concatenate([a, a], axis=1)
    return two(cb), two(sb), two(ca), two(sa)


def _permute_in_columns(w):
    sizes = (NA_W, NA_W, NA_W, DIL_W, DIL_W, DIL_W, GQ_W, GKV_W, GKV_W)
    offs = [0]
    for s in sizes:
        offs.append(offs[-1] + s)
    seg = [w[:, offs[i]:offs[i + 1]] for i in range(len(sizes))]
    qa, ka, va, qd, kd, vd, qc, kc, vc = seg
    return jnp.concatenate([qa, ka, qd, kd, qc, kc, va, vd, vc], axis=1)


def _gain_vector(qg, kg):
    scale = HEAD_DIM ** -0.5
    parts = [
        jnp.tile(qg[0] * (scale * LOG2E), H_NA), jnp.tile(kg[0], H_NA),
        jnp.tile(qg[1] * (scale * LOG2E), H_DIL), jnp.tile(kg[1], H_DIL),
        jnp.tile(qg[2] * (scale * LOG2E), H_GQ), jnp.tile(kg[2], H_GKV),
    ]
    return jnp.concatenate(parts)[None, :]


def kernel(x, norm1_g, w_in, q_norm_g, k_norm_g, rpb, out_norm_g, w_out, norm2_g,
           w_gate_up, conv_w, conv_b, w_down):
    B, S, D = x.shape
    depth = w_in.shape[0]
    N = B * S
    assert S == GRID_W * GRID_W and S % TM_PROJ == 0 and S % TQ_GQA == 0 and S % TK_GQA == 0
    tabs = _rotary_tables(S)
    idx = jnp.arange(MXU_N) // HEAD_DIM
    gsum = (idx[:, None] == idx[None, :]).astype(BF)
    x2 = x.reshape(N, D)
    for l in range(depth):
        w_perm = _permute_in_columns(w_in[l].astype(BF))
        gain = _gain_vector(q_norm_g[l], k_norm_g[l])
        qa, ka, va, qd, kd, vd, qc, kvc = _inproj(
            x2, norm1_g[l][None, :], w_perm, gsum, gain, tabs, S)
        seq = lambda a: a.reshape(B, S, a.shape[-1])
        out_a = _nbr_attention(seq(qa), seq(ka), seq(va), _bias2(rpb[l]))
        out_d = _dilated_attention(seq(qd), seq(kd), seq(vd))
        out_c = _gqa_attention(seq(qc), seq(kvc))
        x2 = _ffn(out_a.reshape(N, NA_W), out_d.reshape(N, DIL_W), out_c.reshape(N, GQ_W), x2,
                  out_norm_g[l][None, :], w_out[l].astype(BF), norm2_g[l][None, :],
                  w_gate_up[l].astype(BF), conv_w[l], conv_b[l][None, :], w_down[l].astype(BF), S)
    return x2.reshape(B, S, D)
```

```python
import functools

import jax
import jax.numpy as jnp
from jax import lax
from jax.experimental import pallas as pl
from jax.experimental.pallas import tpu as pltpu

BF = jnp.bfloat16
F32 = jnp.float32

HEAD_DIM = 64
H_NA, H_DIL, H_GQ, H_GKV = 4, 6, 6, 2
NA_W, DIL_W, GQ_W, GKV_W = 256, 384, 384, 128
GRID_W = 64
NA_KH = 8
NA_KW = 16
DIL_PATTERNS = ((128, 1), (512, 4), (2048, 16))
BAND_HALF = 64
ROPE_THETA = 500000.0
ROPE_DIMS = 16
AXIAL_THETA = 10000.0
CONV_W = 3
EPS = 1e-6
NEG_INF = -1e30
LOG2E = 1.4426950408889634

LANES = 128
MXU_N = 256
VMEM_LIMIT = 56 * 1024 * 1024

TM_PROJ = 512
TM_IN = 1024
PROJ_SUB = 256
TM_FFN = 512
FFN_HALO = 16
TQ_BAND = 128
BAND_WIN = 256
TQ_GQA = 256
TK_GQA = 2048


def _cparams(n_axes):
    return pltpu.CompilerParams(
        dimension_semantics=("arbitrary",) * n_axes, vmem_limit_bytes=VMEM_LIMIT)


def _rms(v, g):
    ms = jnp.mean(v * v, axis=-1, keepdims=True)
    return v * lax.rsqrt(ms + EPS) * g


def _resident(shape):
    nd = len(shape)
    return pl.BlockSpec(shape, lambda *_: (0,) * nd, pipeline_mode=pl.Buffered(1))


def _resident_layer(stacked_shape, l):
    nd = len(stacked_shape)
    return pl.BlockSpec((None,) + tuple(stacked_shape[1:]), lambda *_: (l,) + (0,) * (nd - 1),
                        pipeline_mode=pl.Buffered(1))


def _rotary(p, cos, sin, first, half):
    up = pltpu.roll(p, LANES - half, axis=1)
    dn = pltpu.roll(p, half, axis=1)
    return p * cos + jnp.where(first, up, dn) * sin


def _inproj_body(x_ref, g1_ref, w_ref, gs_ref, gain_ref, cb_ref, sb_ref, cc_ref, sc_ref,
                 qa_ref, ka_ref, va_ref, qd_ref, kd_ref, vd_ref, qc_ref, kvc_ref):
    lane = lax.broadcasted_iota(jnp.int32, (1, LANES), 1)
    first_b = (lane % HEAD_DIM) < (ROPE_DIMS // 2)
    first_c = (lane % (HEAD_DIM // 2)) < (HEAD_DIM // 4)
    lo_half = lane < HEAD_DIM
    dests = (
        ((qa_ref, 0), (qa_ref, 128)),
        ((ka_ref, 0), (ka_ref, 128)),
        ((qd_ref, 0), (qd_ref, 128)),
        ((qd_ref, 256), (kd_ref, 0)),
        ((kd_ref, 128), (kd_ref, 256)),
        ((qc_ref, 0), (qc_ref, 128)),
        ((qc_ref, 256), (kvc_ref, 0)),
        ((va_ref, 0), (va_ref, 128)),
        ((vd_ref, 0), (vd_ref, 128)),
        ((vd_ref, 256), (kvc_ref, 128)),
    )
    for r0 in range(0, x_ref.shape[0], PROJ_SUB):
        rows = slice(r0, r0 + PROJ_SUB)
        h = _rms(x_ref[rows, :], g1_ref[...]).astype(BF)
        n_norm = gain_ref.shape[1]
        pq = jnp.dot(h, w_ref[:, :n_norm], preferred_element_type=F32)
        pv = jnp.dot(h, w_ref[:, n_norm:], preferred_element_type=F32)
        sq = (pq * pq).astype(BF)
        for c in range(10):
            cols = slice(c * MXU_N, (c + 1) * MXU_N)
            if c < 7:
                ss = jnp.dot(sq[:, cols], gs_ref[...], preferred_element_type=F32)
                p = pq[:, cols] * lax.rsqrt(ss * (1.0 / HEAD_DIM) + EPS) * gain_ref[:, cols]
            else:
                p = pv[:, (c - 7) * MXU_N:(c - 6) * MXU_N]
            halves = [p[:, :LANES], p[:, LANES:]]
            if 2 <= c <= 4:
                halves = [_rotary(v, cb_ref[rows, :], sb_ref[rows, :], first_b, ROPE_DIMS // 2)
                          for v in halves]
            elif 5 <= c <= 6:
                halves = [_rotary(v, cc_ref[rows, :], sc_ref[rows, :], first_c, HEAD_DIM // 4)
                          for v in halves]
            for v, (ref, off) in zip(halves, dests[c]):
                if ref is kvc_ref and off == LANES:
                    ref[rows, LANES:2 * LANES] = jnp.where(lo_half, v, 1.0).astype(BF)
                    ref[rows, 2 * LANES:3 * LANES] = jnp.where(lo_half, 1.0, v).astype(BF)
                else:
                    ref[rows, off:off + LANES] = v.astype(BF)


def _inproj(x2, g1, w_perm, l, gsum, gain, tabs, S):
    N, D = x2.shape
    tm = TM_IN
    assert S % tm == 0 and tm % PROJ_SUB == 0
    nt_seq = S // tm
    row = lambda i: (i, 0)
    tab = pl.BlockSpec((tm, LANES), lambda i: (i % nt_seq, 0))
    widths = (NA_W, NA_W, NA_W, DIL_W, DIL_W, DIL_W, GQ_W, 3 * GKV_W)
    return pl.pallas_call(
        _inproj_body,
        grid=(N // tm,),
        in_specs=[
            pl.BlockSpec((tm, D), row),
            _resident(g1.shape),
            _resident_layer(w_perm.shape, l),
            _resident(gsum.shape),
            _resident(gain.shape),
            tab, tab, tab, tab,
        ],
        out_specs=[pl.BlockSpec((tm, w), row) for w in widths],
        out_shape=[jax.ShapeDtypeStruct((N, w), BF) for w in widths],
        compiler_params=_cparams(1),
        name="inproj",
    )(x2, g1, w_perm, gsum, gain, *tabs)


DIL_STRIDES = tuple(d for _, d in DIL_PATTERNS)


def _band_bias():
    r = jnp.arange(TQ_BAND)[:, None]
    c = jnp.arange(BAND_WIN)[None, :]
    masks = [jnp.where(jnp.abs(r + off - c) <= BAND_HALF, 0.0, NEG_INF).astype(F32)
             for off in (0, BAND_HALF, 2 * BAND_HALF)]
    return jnp.stack([jnp.concatenate([m, m], axis=0) for m in masks])


def _band_tile(q, kwin, vwin_a, vwin_b, bias, lo):
    tq = TQ_BAND
    zero = jnp.zeros_like(q)
    qs = jnp.concatenate([jnp.where(lo, q, zero), jnp.where(lo, zero, q)], axis=0)
    s = lax.dot_general(qs, kwin, (((1,), (1,)), ((), ())), preferred_element_type=F32) + bias
    m = jnp.max(s, axis=1, keepdims=True)
    p = jnp.exp2(s - m).astype(BF)
    acc_a = jnp.dot(p[:tq], vwin_a, preferred_element_type=F32)
    acc_b = jnp.dot(p[tq:], vwin_b, preferred_element_type=F32)
    m_rep = jnp.broadcast_to(m, (2 * tq, LANES))
    return (acc_a, acc_b), (m_rep[:tq], m_rep[tq:])


def _dil_body(q_ref, k_ref, v_ref, bias_ref, o_ref,
              stage, stage4, kc, vac, vbc, qc, acc, mx, *, S):
    tq, win = TQ_BAND, BAND_WIN
    half = S // 2
    hf = pl.program_id(2)
    lo = lax.broadcasted_iota(jnp.int32, (tq, LANES), 1) < HEAD_DIM

    def deinterleave(dst, g0, rows):
        n4, n16 = rows // 4, rows // 16
        for r in range(4):
            cls = stage[pl.ds(r, n4, stride=4), :]
            stage4[r * n4:(r + 1) * n4, :] = cls
            dst[g0, r * n4:(r + 1) * n4, :] = cls.astype(BF)
        for r in range(16):
            cls = stage4[pl.ds((r % 4) * n4 + r // 4, n16, stride=4), :]
            dst[g0 + 1, r * n16:(r + 1) * n16, :] = cls.astype(BF)

    @pl.when(hf == 0)
    def _():
        lo_s = lax.broadcasted_iota(jnp.int32, (S, LANES), 1) < HEAD_DIM
        stage[...] = k_ref[0].astype(F32)
        deinterleave(kc, 0, S)
        v = v_ref[0].astype(F32)
        stage[...] = jnp.where(lo_s, v, 1.0)
        vac[0] = stage[...].astype(BF)
        deinterleave(vac, 1, S)
        stage[...] = jnp.where(lo_s, 1.0, v)
        vbc[0] = stage[...].astype(BF)
        deinterleave(vbc, 1, S)

    stage[0:half] = q_ref[0].astype(F32)
    deinterleave(qc, 0, half)

    for g, d in enumerate(DIL_STRIDES):
        L = S // d
        n = half // d
        for r in range(d):
            for jj in range(n // tq):
                i0 = hf * n + jj * tq
                start = jnp.clip(i0 - BAND_HALF, 0, L - win)
                kv_rows = pl.ds(pl.multiple_of(r * L + start, BAND_HALF), win)
                if g == 0:
                    q = q_ref[0, jj * tq:(jj + 1) * tq, :]
                    kwin, va, vb = k_ref[0, kv_rows, :], vac[0, kv_rows, :], vbc[0, kv_rows, :]
                    rows = pl.ds(jj * tq, tq)
                else:
                    q = qc[g - 1, r * n + jj * tq:r * n + (jj + 1) * tq, :]
                    kwin, va, vb = kc[g - 1, kv_rows, :], vac[g, kv_rows, :], vbc[g, kv_rows, :]
                    rows = pl.ds(jj * tq * d + r, tq, stride=d)
                bias = bias_ref[(i0 - start) // BAND_HALF]
                accs, ms = _band_tile(q, kwin, va, vb, bias, lo)
                for hd in range(2):
                    acc[g, hd, rows, :] = accs[hd]
                    mx[g, hd, rows, :] = ms[hd]

    chunk = 2 * tq
    lo_c = lax.broadcasted_iota(jnp.int32, (chunk, LANES), 1) < HEAD_DIM
    n_pat = len(DIL_STRIDES)
    for c in range(half // chunk):
        rows = slice(c * chunk, (c + 1) * chunk)
        outs = []
        for hd in range(2):
            m = [mx[g, hd, rows, :] for g in range(n_pat)]
            m_all = functools.reduce(jnp.maximum, m)
            tot = sum(jnp.exp2(m[g] - m_all) * acc[g, hd, rows, :] for g in range(n_pat))
            outs.append(tot * (1.0 / pltpu.roll(tot, HEAD_DIM, axis=1)))
        o_ref[0, rows, :] = jnp.where(lo_c, outs[0], outs[1]).astype(BF)


def _dilated_attention(qd, kd, vd):
    B, S, W = qd.shape
    assert all((w // 2) // d == BAND_HALF for w, d in DIL_PATTERNS)
    assert DIL_STRIDES == (1, 4, 16) and S // DIL_STRIDES[-1] == BAND_WIN
    assert (S // 2) % (TQ_BAND * DIL_STRIDES[-1]) == 0
    half = S // 2
    bias = _band_bias()
    n_pat = len(DIL_STRIDES)
    return pl.pallas_call(
        functools.partial(_dil_body, S=S),
        grid=(B, W // LANES, 2),
        in_specs=[
            pl.BlockSpec((1, half, LANES), lambda b, p, h: (b, h, p)),
            pl.BlockSpec((1, S, LANES), lambda b, p, h: (b, 0, p)),
            pl.BlockSpec((1, S, LANES), lambda b, p, h: (b, 0, p)),
            _resident(bias.shape),
        ],
        out_specs=pl.BlockSpec((1, half, LANES), lambda b, p, h: (b, h, p)),
        out_shape=jax.ShapeDtypeStruct((B, S, W), BF),
        scratch_shapes=[
            pltpu.VMEM((S, LANES), F32),
            pltpu.VMEM((S, LANES), F32),
            pltpu.VMEM((n_pat - 1, S, LANES), BF),
            pltpu.VMEM((n_pat, S, LANES), BF),
            pltpu.VMEM((n_pat, S, LANES), BF),
            pltpu.VMEM((n_pat - 1, half, LANES), BF),
            pltpu.VMEM((n_pat, 2, half, LANES), F32),
            pltpu.VMEM((n_pat, 2, half, LANES), F32),
        ],
        compiler_params=_cparams(3),
        name="dilated_attn",
    )(qd, kd, vd, bias)


NA_ROWS_WIN = 10
NA_DR = 2 * NA_KH - 1
NA_DC = 2 * NA_KW - 1


def _bias2_body(rpb_ref, o_ref):
    cq = lax.broadcasted_iota(jnp.int32, (GRID_W, LANES), 0)
    lane = lax.broadcasted_iota(jnp.int32, (GRID_W, LANES), 1)
    kc = lane % GRID_W
    hi = lane >= GRID_W
    c0 = jnp.clip(cq - NA_KW // 2, 0, GRID_W - NA_KW)
    col_ok = (kc >= c0) & (kc < c0 + NA_KW)
    for h in range(H_NA):
        for d in range(NA_DR + 1):
            rows = [jnp.broadcast_to(rpb_ref[h, d + k:d + k + 1, :], (GRID_W, LANES)) for k in (0, 1)]
            a = pltpu.roll(rows[0], LANES - (NA_KW - 1), axis=1, stride=1, stride_axis=0)
            b = pltpu.roll(rows[1], GRID_W - (NA_KW - 1), axis=1, stride=1, stride_axis=0)
            ok = col_ok
            if d == 0:
                ok = col_ok & hi
            elif d == NA_DR:
                ok = col_ok & jnp.logical_not(hi)
            o_ref[h, d] = jnp.where(ok, jnp.where(hi, b, a) * LOG2E, NEG_INF)


def _bias2(rpb_l):
    h, ndr, ndc = rpb_l.shape
    rpb_pad = jnp.pad(rpb_l, ((0, 0), (1, 1), (0, LANES - ndc)))
    shape = (H_NA, NA_DR + 1, GRID_W, LANES)
    return pl.pallas_call(
        _bias2_body,
        grid=(1,),
        in_specs=[_resident(rpb_pad.shape)],
        out_specs=pl.BlockSpec(shape, lambda i: (0, 0, 0, 0)),
        out_shape=jax.ShapeDtypeStruct(shape, F32),
        compiler_params=_cparams(1),
        name="nbr_bias",
    )(rpb_pad)


NA_TILE_VARIANTS = 5
NA_TILES_PER_STEP = 4


def _nbr_window_row(j, rows):
    return jnp.clip(2 * j - NA_KH // 2, 0, rows - NA_ROWS_WIN)


def _tile_bias_body(b2_ref, o_ref, *, rows):
    v = pl.program_id(0)
    h = pl.program_id(1)
    n_tiles = rows // 2
    j = jnp.where(v < 2, v, jnp.where(v == 2, 2, v + (n_tiles - NA_TILE_VARIANTS)))
    rs = _nbr_window_row(j, rows)
    lane_b = lax.broadcasted_iota(jnp.int32, (GRID_W, LANES), 1) // GRID_W
    bias_rows = []
    for qi in range(2):
        rq = 2 * j + qi
        r0 = jnp.clip(rq - NA_KH // 2, 0, rows - NA_KH)
        blocks = []
        for kk in range(NA_ROWS_WIN // 2):
            kr = rs + 2 * kk
            d = jnp.clip(kr - rq + NA_KH, 0, NA_DR)
            krl = kr + lane_b
            ok = (krl >= r0) & (krl < r0 + NA_KH)
            blocks.append(jnp.where(ok, b2_ref[h, d], NEG_INF))
        bias_rows.append(jnp.concatenate(blocks, axis=1))
    o_ref[0, 0] = jnp.concatenate(bias_rows, axis=0)


def _tile_bias(bias2, rows):
    assert rows // 2 > NA_TILE_VARIANTS
    win = NA_ROWS_WIN * GRID_W
    return pl.pallas_call(
        functools.partial(_tile_bias_body, rows=rows),
        grid=(NA_TILE_VARIANTS, H_NA),
        in_specs=[_resident(bias2.shape)],
        out_specs=pl.BlockSpec((1, 1, TQ_BAND, win), lambda v, h: (v, h, 0, 0)),
        out_shape=jax.ShapeDtypeStruct((NA_TILE_VARIANTS, H_NA, TQ_BAND, win), F32),
        compiler_params=_cparams(2),
        name="nbr_tile_bias",
    )(bias2)


def _nbr_body(q_ref, k_ref, v_ref, bias_ref, o_ref, va, vb, *, rows):
    tq = TQ_BAND
    win = NA_ROWS_WIN * GRID_W
    n_tiles = rows // 2
    S = rows * GRID_W
    jt = pl.program_id(1)

    @pl.when(jt == 0)
    def _():
        lo_s = lax.broadcasted_iota(jnp.int32, (S, LANES), 1) < HEAD_DIM
        for hp in range(H_NA // 2):
            v = v_ref[0, :, hp * LANES:(hp + 1) * LANES]
            va[hp] = jnp.where(lo_s, v, jnp.ones_like(v))
            vb[hp] = jnp.where(lo_s, jnp.ones_like(v), v)

    lo = lax.broadcasted_iota(jnp.int32, (tq, LANES), 1) < HEAD_DIM
    for t in range(NA_TILES_PER_STEP):
        j = jt * NA_TILES_PER_STEP + t
        start = pl.multiple_of(_nbr_window_row(j, rows) * GRID_W, 2 * GRID_W)
        var = jnp.where(j < 2, j, jnp.where(j >= n_tiles - 2, j - (n_tiles - NA_TILE_VARIANTS), 2))
        for hp in range(H_NA // 2):
            cols = slice(hp * LANES, (hp + 1) * LANES)
            q = q_ref[0, t * tq:(t + 1) * tq, cols]
            zero = jnp.zeros_like(q)
            qs = jnp.concatenate([jnp.where(lo, q, zero), jnp.where(lo, zero, q)], axis=0)
            kw = k_ref[0, pl.ds(start, win), cols]
            bias = jnp.concatenate([bias_ref[var, 2 * hp], bias_ref[var, 2 * hp + 1]], axis=0)
            s = lax.dot_general(qs, kw, (((1,), (1,)), ((), ())),
                                preferred_element_type=F32) + bias
            m = jnp.max(s, axis=1, keepdims=True)
            p = jnp.exp2(s - m).astype(BF)
            outs = []
            for hh, vref in enumerate((va, vb)):
                a = jnp.dot(p[hh * tq:(hh + 1) * tq], vref[hp, pl.ds(start, win), :],
                            preferred_element_type=F32)
                outs.append(a * (1.0 / pltpu.roll(a, HEAD_DIM, axis=1)))
            o_ref[0, t * tq:(t + 1) * tq, cols] = jnp.where(lo, outs[0], outs[1]).astype(BF)


def _nbr_attention(qa, ka, va, bias2):
    B, S, W = qa.shape
    rows = S // GRID_W
    tq = TQ_BAND * NA_TILES_PER_STEP
    assert S % tq == 0
    bias_full = _tile_bias(bias2, rows)
    qblk = pl.BlockSpec((1, tq, W), lambda b, j: (b, j, 0))
    kblk = pl.BlockSpec((1, S, W), lambda b, j: (b, 0, 0))
    return pl.pallas_call(
        functools.partial(_nbr_body, rows=rows),
        grid=(B, S // tq),
        in_specs=[qblk, kblk, kblk, _resident(bias_full.shape)],
        out_specs=qblk,
        scratch_shapes=[pltpu.VMEM((H_NA // 2, S, LANES), BF) for _ in range(2)],
        out_shape=jax.ShapeDtypeStruct((B, S, W), BF),
        compiler_params=_cparams(2),
        name="nbr_attn",
    )(qa, ka, va, bias_full)


def _gqa_body(q_ref, kv_ref, o_ref, *, S):
    tq, tk = TQ_GQA, TK_GQA
    g = H_GQ // H_GKV
    lane = lax.broadcasted_iota(jnp.int32, (tq, LANES), 1)
    lo = lane < HEAD_DIM
    qs = []
    for kvh in range(H_GKV):
        mine = lo if kvh == 0 else jnp.logical_not(lo)
        parts = []
        for i in range(g):
            h = g * kvh + i
            chunk = q_ref[0, :, (h // 2) * LANES:(h // 2 + 1) * LANES]
            if (h % 2) != kvh:
                chunk = pltpu.roll(chunk.astype(F32), HEAD_DIM, axis=1).astype(BF)
            parts.append(jnp.where(mine, chunk, jnp.zeros_like(chunk)))
        qs.append(jnp.concatenate(parts, axis=0))

    def step(c, carry):
        rows = pl.ds(pl.multiple_of(c * tk, tk), tk)
        kblk = kv_ref[0, rows, 0:LANES]
        new = []
        for kvh in range(H_GKV):
            m, acc = carry[kvh]
            vblk = kv_ref[0, rows, (1 + kvh) * LANES:(2 + kvh) * LANES]
            s = lax.dot_general(qs[kvh], kblk, (((1,), (1,)), ((), ())),
                                preferred_element_type=F32)
            m_new = jnp.maximum(m, jnp.max(s, axis=1, keepdims=True))
            p = jnp.exp2(s - m_new)
            acc = jnp.exp2(m - m_new) * acc + jnp.dot(p.astype(BF), vblk,
                                                      preferred_element_type=F32)
            new.append((m_new, acc))
        return tuple(new)

    init = tuple((jnp.full((g * tq, 1), NEG_INF, F32), jnp.zeros((g * tq, LANES), F32))
                 for _ in range(H_GKV))
    fin = lax.fori_loop(0, S // tk, step, init, unroll=True)
    res = [acc * (1.0 / pltpu.roll(acc, HEAD_DIM, axis=1)) for _, acc in fin]
    for c in range(GQ_W // LANES):
        halves = []
        for hh in range(2):
            h = 2 * c + hh
            kvh, i = h // g, h % g
            v = res[kvh][i * tq:(i + 1) * tq]
            if kvh != hh:
                v = pltpu.roll(v, HEAD_DIM, axis=1)
            halves.append(v)
        o_ref[0, :, c * LANES:(c + 1) * LANES] = jnp.where(lo, halves[0], halves[1]).astype(BF)


def _gqa_attention(qc, kvc):
    B, S, W = qc.shape
    qblk = pl.BlockSpec((1, TQ_GQA, W), lambda b, j: (b, j, 0))
    kvblk = pl.BlockSpec((1, S, kvc.shape[-1]), lambda b, j: (b, 0, 0))
    return pl.pallas_call(
        functools.partial(_gqa_body, S=S),
        grid=(B, S // TQ_GQA),
        in_specs=[qblk, kvblk],
        out_specs=qblk,
        out_shape=jax.ShapeDtypeStruct((B, S, W), BF),
        compiler_params=_cparams(2),
        name="gqa_attn",
    )(qc, kvc)


def _ffn_body(*refs, tiles_per_seq, d_ff):
    (oa, oa_p, oa_n, od, od_p, od_n, oc, oc_p, oc_n, x, x_p, x_n,
     og_ref, wo_ref, g2_ref, wgu_ref, cw_ref, cb_ref, wd_ref, o_ref, hext, gs, act) = refs
    tm, halo = TM_FFN, FFN_HALO
    i = pl.program_id(0) % tiles_per_seq

    def ext(main, prev, nxt):
        return jnp.concatenate([prev[...], main[...], nxt[...]], axis=0)

    na = _rms(ext(oa, oa_p, oa_n).astype(F32), og_ref[:, :NA_W])
    nd = _rms(ext(od, od_p, od_n).astype(F32), og_ref[:, NA_W:NA_W + DIL_W])
    nc = _rms(ext(oc, oc_p, oc_n).astype(F32), og_ref[:, NA_W + DIL_W:])
    mix = jnp.concatenate([na, nd, nc], axis=1).astype(BF)
    y_ext = ext(x, x_p, x_n) + jnp.dot(mix, wo_ref[...], preferred_element_type=F32)
    y = y_ext[halo:halo + tm]
    row = lax.broadcasted_iota(jnp.int32, (tm + 2 * halo, 1), 0)
    keep = ((row >= halo) | (i > 0)) & ((row < halo + tm) | (i < tiles_per_seq - 1))
    hext[...] = jnp.where(keep, _rms(y_ext, g2_ref[...]), 0.0).astype(BF)
    for c in range(d_ff // MXU_N):
        cols = slice(c * MXU_N, (c + 1) * MXU_N)
        ucols = slice(d_ff + c * MXU_N, d_ff + (c + 1) * MXU_N)
        gs[...] = jnp.dot(hext[...], wgu_ref[:, cols], preferred_element_type=F32)
        u = jnp.dot(hext[halo:halo + tm], wgu_ref[:, ucols], preferred_element_type=F32)
        gc = gs[halo - 1:halo - 1 + tm] * cw_ref[0:1, cols] + cb_ref[:, cols]
        gc = gc + gs[halo:halo + tm] * cw_ref[1:2, cols]
        gc = gc + gs[halo + 1:halo + 1 + tm] * cw_ref[2:3, cols]
        gelu = 0.5 * gc * (1.0 + lax.erf(gc * (0.5 ** 0.5)))
        act[:, cols] = (gelu * u).astype(BF)
    o_ref[...] = y + jnp.dot(act[...], wd_ref[...], preferred_element_type=F32)


def _ffn(oa, od, oc, x2, og, w_out, g2, wgu, cw, cb, wd, l, S):
    N, D = x2.shape
    d_ff = wd.shape[1]
    tm, halo = TM_FFN, FFN_HALO
    assert S % tm == 0 and d_ff % MXU_N == 0
    r = tm // halo
    nblk = N // halo
    row = lambda i: (i, 0)
    prev = lambda i: (jnp.maximum(i * r - 1, 0), 0)
    nxt = lambda i: (jnp.minimum((i + 1) * r, nblk - 1), 0)
    tiled, args = [], []
    for a in (oa, od, oc, x2):
        w = a.shape[1]
        tiled += [pl.BlockSpec((tm, w), row), pl.BlockSpec((halo, w), prev),
                  pl.BlockSpec((halo, w), nxt)]
        args += [a, a, a]
    params = (og, w_out, g2, wgu, cw, cb, wd)
    return pl.pallas_call(
        functools.partial(_ffn_body, tiles_per_seq=S // tm, d_ff=d_ff),
        grid=(N // tm,),
        in_specs=tiled + [_resident_layer(p.shape, l) if p.ndim == 3 else _resident(p.shape)
                          for p in params],
        out_specs=pl.BlockSpec((tm, D), row),
        out_shape=jax.ShapeDtypeStruct((N, D), F32),
        scratch_shapes=[
            pltpu.VMEM((tm + 2 * halo, D), BF),
            pltpu.VMEM((tm + 2 * halo, MXU_N), F32),
            pltpu.VMEM((tm, d_ff), BF),
        ],
        compiler_params=_cparams(1),
        name="ffn",
    )(*args, *params)


def _rope_cos_sin(pos, dim, theta):
    inv = theta ** (-jnp.arange(0, dim, 2, dtype=F32) / dim)
    ang = inv[:, None] * pos.astype(F32)[None, :]
    cos, sin = lax.optimization_barrier((jnp.cos(ang), jnp.sin(ang)))
    return cos.T, sin.T


def _rotary_tables(S):
    t = jnp.arange(S, dtype=jnp.int32)
    cos1, sin1 = _rope_cos_sin(t, ROPE_DIMS, ROPE_THETA)
    rest = HEAD_DIM - ROPE_DIMS
    cb = jnp.concatenate([cos1, cos1, jnp.ones((S, rest), F32)], axis=1)
    sb = jnp.concatenate([-sin1, sin1, jnp.zeros((S, rest), F32)], axis=1)
    cg, sg = _rope_cos_sin(jnp.arange(GRID_W, dtype=jnp.int32), HEAD_DIM // 2, AXIAL_THETA)
    cr, sr = (jnp.repeat(a, S // GRID_W, axis=0) for a in (cg, sg))
    cc, sc = (jnp.tile(a, (S // GRID_W, 1)) for a in (cg, sg))
    ca = jnp.concatenate([cr, cr, cc, cc], axis=1)
    sa = jnp.concatenate([-sr, sr, -sc, sc], axis=1)
    two = lambda a: jnp.concatenate([a, a], axis=1)
    return two(cb), two(sb), two(ca), two(sa)


def _permute_in_columns(w):
    sizes = (NA_W, NA_W, NA_W, DIL_W, DIL_W, DIL_W, GQ_W, GKV_W, GKV_W)
    offs = [0]
    for s in sizes:
        offs.append(offs[-1] + s)
    seg = [w[..., offs[i]:offs[i + 1]] for i in range(len(sizes))]
    qa, ka, va, qd, kd, vd, qc, kc, vc = seg
    return jnp.concatenate([qa, ka, qd, kd, qc, kc, va, vd, vc], axis=-1)


def _gain_vector(qg, kg):
    scale = HEAD_DIM ** -0.5
    parts = [
        jnp.tile(qg[0] * (scale * LOG2E), H_NA), jnp.tile(kg[0], H_NA),
        jnp.tile(qg[1] * (scale * LOG2E), H_DIL), jnp.tile(kg[1], H_DIL),
        jnp.tile(qg[2] * (scale * LOG2E), H_GQ), jnp.tile(kg[2], H_GKV),
    ]
    return jnp.concatenate(parts)[None, :]


def kernel(x, norm1_g, w_in, q_norm_g, k_norm_g, rpb, out_norm_g, w_out, norm2_g,
           w_gate_up, conv_w, conv_b, w_down):
    B, S, D = x.shape
    depth = w_in.shape[0]
    N = B * S
    assert S == GRID_W * GRID_W and S % TM_PROJ == 0 and S % TQ_GQA == 0 and S % TK_GQA == 0
    tabs = _rotary_tables(S)
    idx = jnp.arange(MXU_N) // HEAD_DIM
    gsum = (idx[:, None] == idx[None, :]).astype(BF)
    x2 = x.reshape(N, D)
    w_perm = _permute_in_columns(w_in.astype(BF))
    w_out, w_gate_up, w_down = (w.astype(BF) for w in (w_out, w_gate_up, w_down))
    for l in range(depth):
        gain = _gain_vector(q_norm_g[l], k_norm_g[l])
        qa, ka, va, qd, kd, vd, qc, kvc = _inproj(
            x2, norm1_g[l][None, :], w_perm, l, gsum, gain, tabs, S)
        seq = lambda a: a.reshape(B, S, a.shape[-1])
        out_a = _nbr_attention(seq(qa), seq(ka), seq(va), _bias2(rpb[l]))
        out_d = _dilated_attention(seq(qd), seq(kd), seq(vd))
        out_c = _gqa_attention(seq(qc), seq(kvc))
        x2 = _ffn(out_a.reshape(N, NA_W), out_d.reshape(N, DIL_W), out_c.reshape(N, GQ_W), x2,
                  out_norm_g[l][None, :], w_out, norm2_g[l][None, :],
                  w_gate_up, conv_w[l], conv_b[l][None, :], w_down, l, S)
    return x2.reshape(B, S, D)
```

```python
import functools

import jax
import jax.numpy as jnp
from jax import lax
from jax.experimental import pallas as pl
from jax.experimental.pallas import tpu as pltpu

BF = jnp.bfloat16
F32 = jnp.float32

HEAD_DIM = 64
H_NA, H_DIL, H_GQ, H_GKV = 4, 6, 6, 2
NA_W, DIL_W, GQ_W, GKV_W = 256, 384, 384, 128
GRID_W = 64
NA_KH = 8
NA_KW = 16
DIL_PATTERNS = ((128, 1), (512, 4), (2048, 16))
BAND_HALF = 64
ROPE_THETA = 500000.0
ROPE_DIMS = 16
AXIAL_THETA = 10000.0
CONV_W = 3
EPS = 1e-6
NEG_INF = -1e30
LOG2E = 1.4426950408889634

LANES = 128
MXU_N = 256
VMEM_LIMIT = 56 * 1024 * 1024

TM_PROJ = 512
TM_IN = 1024
PROJ_SUB = 256
TM_FFN = 512
FFN_HALO = 16
TQ_BAND = 128
BAND_WIN = 256
TQ_GQA = 256
TK_GQA = 2048


def _cparams(n_axes):
    return pltpu.CompilerParams(
        dimension_semantics=("arbitrary",) * n_axes, vmem_limit_bytes=VMEM_LIMIT)


def _rms(v, g):
    ms = jnp.mean(v * v, axis=-1, keepdims=True)
    return v * lax.rsqrt(ms + EPS) * g


def _resident(shape):
    nd = len(shape)
    return pl.BlockSpec(shape, lambda *_: (0,) * nd, pipeline_mode=pl.Buffered(1))


def _resident_layer(stacked_shape, l):
    nd = len(stacked_shape)
    return pl.BlockSpec((None,) + tuple(stacked_shape[1:]), lambda *_: (l,) + (0,) * (nd - 1),
                        pipeline_mode=pl.Buffered(1))


def _rotary(p, cos, sin, first, half):
    up = pltpu.roll(p, LANES - half, axis=1)
    dn = pltpu.roll(p, half, axis=1)
    return p * cos + jnp.where(first, up, dn) * sin


def _inproj_body(x_ref, g1_ref, w_ref, gs_ref, gain_ref, cb_ref, sb_ref, cc_ref, sc_ref,
                 qa_ref, ka_ref, va_ref, qd_ref, kd_ref, vd_ref, qc_ref, kvc_ref):
    lane = lax.broadcasted_iota(jnp.int32, (1, LANES), 1)
    first_b = (lane % HEAD_DIM) < (ROPE_DIMS // 2)
    first_c = (lane % (HEAD_DIM // 2)) < (HEAD_DIM // 4)
    lo_half = lane < HEAD_DIM
    dests = (
        ((qa_ref, 0), (qa_ref, 128)),
        ((ka_ref, 0), (ka_ref, 128)),
        ((qd_ref, 0), (qd_ref, 128)),
        ((qd_ref, 256), (kd_ref, 0)),
        ((kd_ref, 128), (kd_ref, 256)),
        ((qc_ref, 0), (qc_ref, 128)),
        ((qc_ref, 256), (kvc_ref, 0)),
        ((va_ref, 0), (va_ref, 128)),
        ((vd_ref, 0), (vd_ref, 128)),
        ((vd_ref, 256), (kvc_ref, 128)),
    )
    for r0 in range(0, x_ref.shape[0], PROJ_SUB):
        rows = slice(r0, r0 + PROJ_SUB)
        h = _rms(x_ref[rows, :], g1_ref[...]).astype(BF)
        n_norm = gain_ref.shape[1]
        pq = jnp.dot(h, w_ref[:, :n_norm], preferred_element_type=F32)
        pv = jnp.dot(h, w_ref[:, n_norm:], preferred_element_type=F32)
        sq = (pq * pq).astype(BF)
        for c in range(10):
            cols = slice(c * MXU_N, (c + 1) * MXU_N)
            if c < 7:
                ss = jnp.dot(sq[:, cols], gs_ref[...], preferred_element_type=F32)
                p = pq[:, cols] * lax.rsqrt(ss * (1.0 / HEAD_DIM) + EPS) * gain_ref[:, cols]
            else:
                p = pv[:, (c - 7) * MXU_N:(c - 6) * MXU_N]
            halves = [p[:, :LANES], p[:, LANES:]]
            if 2 <= c <= 4:
                halves = [_rotary(v, cb_ref[rows, :], sb_ref[rows, :], first_b, ROPE_DIMS // 2)
                          for v in halves]
            elif 5 <= c <= 6:
                halves = [_rotary(v, cc_ref[rows, :], sc_ref[rows, :], first_c, HEAD_DIM // 4)
                          for v in halves]
            for v, (ref, off) in zip(halves, dests[c]):
                if ref is kvc_ref and off == LANES:
                    ref[rows, LANES:2 * LANES] = jnp.where(lo_half, v, 1.0).astype(BF)
                    ref[rows, 2 * LANES:3 * LANES] = jnp.where(lo_half, 1.0, v).astype(BF)
                else:
                    ref[rows, off:off + LANES] = v.astype(BF)


def _inproj(x2, g1, w_perm, l, gsum, gain, tabs, S):
    N, D = x2.shape
    tm = TM_IN
    assert S % tm == 0 and tm % PROJ_SUB == 0
    nt_seq = S // tm
    row = lambda i: (i, 0)
    tab = pl.BlockSpec((tm, LANES), lambda i: (i % nt_seq, 0))
    widths = (NA_W, NA_W, NA_W, DIL_W, DIL_W, DIL_W, GQ_W, 3 * GKV_W)
    return pl.pallas_call(
        _inproj_body,
        grid=(N // tm,),
        in_specs=[
            pl.BlockSpec((tm, D), row),
            _resident(g1.shape),
            _resident_layer(w_perm.shape, l),
            _resident(gsum.shape),
            _resident(gain.shape),
            tab, tab, tab, tab,
        ],
        out_specs=[pl.BlockSpec((tm, w), row) for w in widths],
        out_shape=[jax.ShapeDtypeStruct((N, w), BF) for w in widths],
        compiler_params=_cparams(1),
        name="inproj",
    )(x2, g1, w_perm, gsum, gain, *tabs)


DIL_STRIDES = tuple(d for _, d in DIL_PATTERNS)


def _band_bias():
    r = jnp.arange(TQ_BAND)[:, None]
    c = jnp.arange(BAND_WIN)[None, :]
    masks = [jnp.where(jnp.abs(r + off - c) <= BAND_HALF, 0.0, NEG_INF).astype(F32)
             for off in (0, BAND_HALF, 2 * BAND_HALF)]
    return jnp.stack(masks)


def _dil_body(q_ref, k_ref, v_ref, bias_ref, o_ref,
              stage, stage4, kc, vac, vbc, qc, acc, mx, *, S):
    tq, win = TQ_BAND, BAND_WIN
    half = S // 2
    hf = pl.program_id(2)
    lo = lax.broadcasted_iota(jnp.int32, (tq, LANES), 1) < HEAD_DIM

    def deinterleave(dst, g0, rows):
        n4, n16 = rows // 4, rows // 16
        for r in range(4):
            cls = stage[pl.ds(r, n4, stride=4), :]
            stage4[r * n4:(r + 1) * n4, :] = cls
            dst[g0, r * n4:(r + 1) * n4, :] = cls.astype(BF)
        for r in range(16):
            cls = stage4[pl.ds((r % 4) * n4 + r // 4, n16, stride=4), :]
            dst[g0 + 1, r * n16:(r + 1) * n16, :] = cls.astype(BF)

    @pl.when(hf == 0)
    def _():
        lo_s = lax.broadcasted_iota(jnp.int32, (S, LANES), 1) < HEAD_DIM
        stage[...] = k_ref[0].astype(F32)
        deinterleave(kc, 0, S)
        v = v_ref[0].astype(F32)
        stage[...] = jnp.where(lo_s, v, 1.0)
        vac[0] = stage[...].astype(BF)
        deinterleave(vac, 1, S)
        stage[...] = jnp.where(lo_s, 1.0, v)
        vbc[0] = stage[...].astype(BF)
        deinterleave(vbc, 1, S)

    stage[0:half] = q_ref[0].astype(F32)
    deinterleave(qc, 0, half)

    for g, d in enumerate(DIL_STRIDES):
        L = S // d
        n = half // d
        for r in range(d):
            for jj in range(n // tq):
                i0 = hf * n + jj * tq
                start = jnp.clip(i0 - BAND_HALF, 0, L - win)
                kv_rows = pl.ds(pl.multiple_of(r * L + start, BAND_HALF), win)
                if g == 0:
                    q = q_ref[0, jj * tq:(jj + 1) * tq, :]
                    kwin = k_ref[0, kv_rows, :]
                    rows = pl.ds(jj * tq, tq)
                else:
                    q = qc[g - 1, r * n + jj * tq:r * n + (jj + 1) * tq, :]
                    kwin = kc[g - 1, kv_rows, :]
                    rows = pl.ds(jj * tq * d + r, tq, stride=d)
                bias = bias_ref[(i0 - start) // BAND_HALF]
                zero = jnp.zeros_like(q)
                for hd, vref in enumerate((vac, vbc)):
                    qm = jnp.where(lo, q, zero) if hd == 0 else jnp.where(lo, zero, q)
                    s = lax.dot_general(qm, kwin, (((1,), (1,)), ((), ())),
                                        preferred_element_type=F32) + bias
                    m = jnp.max(s, axis=1, keepdims=True)
                    p = jnp.exp2(s - m).astype(BF)
                    acc[g, hd, rows, :] = jnp.dot(p, vref[g, kv_rows, :],
                                                  preferred_element_type=F32)
                    mx[g, hd, rows, :] = jnp.broadcast_to(m, (tq, LANES))

    chunk = 2 * tq
    lo_c = lax.broadcasted_iota(jnp.int32, (chunk, LANES), 1) < HEAD_DIM
    n_pat = len(DIL_STRIDES)
    for c in range(half // chunk):
        rows = slice(c * chunk, (c + 1) * chunk)
        outs = []
        for hd in range(2):
            m = [mx[g, hd, rows, :] for g in range(n_pat)]
            m_all = functools.reduce(jnp.maximum, m)
            tot = sum(jnp.exp2(m[g] - m_all) * acc[g, hd, rows, :] for g in range(n_pat))
            outs.append(tot * (1.0 / pltpu.roll(tot, HEAD_DIM, axis=1)))
        o_ref[0, rows, :] = jnp.where(lo_c, outs[0], outs[1]).astype(BF)


def _dilated_attention(qd, kd, vd):
    B, S, W = qd.shape
    assert all((w // 2) // d == BAND_HALF for w, d in DIL_PATTERNS)
    assert DIL_STRIDES == (1, 4, 16) and S // DIL_STRIDES[-1] == BAND_WIN
    assert (S // 2) % (TQ_BAND * DIL_STRIDES[-1]) == 0
    half = S // 2
    bias = _band_bias()
    n_pat = len(DIL_STRIDES)
    return pl.pallas_call(
        functools.partial(_dil_body, S=S),
        grid=(B, W // LANES, 2),
        in_specs=[
            pl.BlockSpec((1, half, LANES), lambda b, p, h: (b, h, p)),
            pl.BlockSpec((1, S, LANES), lambda b, p, h: (b, 0, p)),
            pl.BlockSpec((1, S, LANES), lambda b, p, h: (b, 0, p)),
            _resident(bias.shape),
        ],
        out_specs=pl.BlockSpec((1, half, LANES), lambda b, p, h: (b, h, p)),
        out_shape=jax.ShapeDtypeStruct((B, S, W), BF),
        scratch_shapes=[
            pltpu.VMEM((S, LANES), F32),
            pltpu.VMEM((S, LANES), F32),
            pltpu.VMEM((n_pat - 1, S, LANES), BF),
            pltpu.VMEM((n_pat, S, LANES), BF),
            pltpu.VMEM((n_pat, S, LANES), BF),
            pltpu.VMEM((n_pat - 1, half, LANES), BF),
            pltpu.VMEM((n_pat, 2, half, LANES), F32),
            pltpu.VMEM((n_pat, 2, half, LANES), F32),
        ],
        compiler_params=_cparams(3),
        name="dilated_attn",
    )(qd, kd, vd, bias)


NA_ROWS_WIN = 10
NA_DR = 2 * NA_KH - 1
NA_DC = 2 * NA_KW - 1


def _bias2_body(rpb_ref, o_ref):
    cq = lax.broadcasted_iota(jnp.int32, (GRID_W, LANES), 0)
    lane = lax.broadcasted_iota(jnp.int32, (GRID_W, LANES), 1)
    kc = lane % GRID_W
    hi = lane >= GRID_W
    c0 = jnp.clip(cq - NA_KW // 2, 0, GRID_W - NA_KW)
    col_ok = (kc >= c0) & (kc < c0 + NA_KW)
    for h in range(H_NA):
        for d in range(NA_DR + 1):
            rows = [jnp.broadcast_to(rpb_ref[h, d + k:d + k + 1, :], (GRID_W, LANES)) for k in (0, 1)]
            a = pltpu.roll(rows[0], LANES - (NA_KW - 1), axis=1, stride=1, stride_axis=0)
            b = pltpu.roll(rows[1], GRID_W - (NA_KW - 1), axis=1, stride=1, stride_axis=0)
            ok = col_ok
            if d == 0:
                ok = col_ok & hi
            elif d == NA_DR:
                ok = col_ok & jnp.logical_not(hi)
            o_ref[h, d] = jnp.where(ok, jnp.where(hi, b, a) * LOG2E, NEG_INF)


def _bias2(rpb_l):
    h, ndr, ndc = rpb_l.shape
    rpb_pad = jnp.pad(rpb_l, ((0, 0), (1, 1), (0, LANES - ndc)))
    shape = (H_NA, NA_DR + 1, GRID_W, LANES)
    return pl.pallas_call(
        _bias2_body,
        grid=(1,),
        in_specs=[_resident(rpb_pad.shape)],
        out_specs=pl.BlockSpec(shape, lambda i: (0, 0, 0, 0)),
        out_shape=jax.ShapeDtypeStruct(shape, F32),
        compiler_params=_cparams(1),
        name="nbr_bias",
    )(rpb_pad)


NA_TILE_VARIANTS = 5
NA_TILES_PER_STEP = 8


def _nbr_window_row(j, rows):
    return jnp.clip(2 * j - NA_KH // 2, 0, rows - NA_ROWS_WIN)


def _tile_bias_body(b2_ref, o_ref, *, rows):
    v = pl.program_id(0)
    h = pl.program_id(1)
    n_tiles = rows // 2
    j = jnp.where(v < 2, v, jnp.where(v == 2, 2, v + (n_tiles - NA_TILE_VARIANTS)))
    rs = _nbr_window_row(j, rows)
    lane_b = lax.broadcasted_iota(jnp.int32, (GRID_W, LANES), 1) // GRID_W
    bias_rows = []
    for qi in range(2):
        rq = 2 * j + qi
        r0 = jnp.clip(rq - NA_KH // 2, 0, rows - NA_KH)
        blocks = []
        for kk in range(NA_ROWS_WIN // 2):
            kr = rs + 2 * kk
            d = jnp.clip(kr - rq + NA_KH, 0, NA_DR)
            krl = kr + lane_b
            ok = (krl >= r0) & (krl < r0 + NA_KH)
            blocks.append(jnp.where(ok, b2_ref[h, d], NEG_INF))
        bias_rows.append(jnp.concatenate(blocks, axis=1))
    o_ref[0, 0] = jnp.concatenate(bias_rows, axis=0)


def _tile_bias(bias2, rows):
    assert rows // 2 > NA_TILE_VARIANTS
    win = NA_ROWS_WIN * GRID_W
    return pl.pallas_call(
        functools.partial(_tile_bias_body, rows=rows),
        grid=(NA_TILE_VARIANTS, H_NA),
        in_specs=[_resident(bias2.shape)],
        out_specs=pl.BlockSpec((1, 1, TQ_BAND, win), lambda v, h: (v, h, 0, 0)),
        out_shape=jax.ShapeDtypeStruct((NA_TILE_VARIANTS, H_NA, TQ_BAND, win), F32),
        compiler_params=_cparams(2),
        name="nbr_tile_bias",
    )(bias2)


def _nbr_body(q_ref, k_ref, v_ref, bias_ref, o_ref, va, vb, *, rows):
    tq = TQ_BAND
    win = NA_ROWS_WIN * GRID_W
    n_tiles = rows // 2
    S = rows * GRID_W
    jt = pl.program_id(1)

    @pl.when(jt == 0)
    def _():
        lo_s = lax.broadcasted_iota(jnp.int32, (S, LANES), 1) < HEAD_DIM
        for hp in range(H_NA // 2):
            v = v_ref[0, :, hp * LANES:(hp + 1) * LANES]
            va[hp] = jnp.where(lo_s, v, jnp.ones_like(v))
            vb[hp] = jnp.where(lo_s, jnp.ones_like(v), v)

    lo = lax.broadcasted_iota(jnp.int32, (tq, LANES), 1) < HEAD_DIM
    jobs = []
    for t in range(NA_TILES_PER_STEP):
        j = jt * NA_TILES_PER_STEP + t
        start = pl.multiple_of(_nbr_window_row(j, rows) * GRID_W, 2 * GRID_W)
        var = jnp.where(j < 2, j, jnp.where(j >= n_tiles - 2, j - (n_tiles - NA_TILE_VARIANTS), 2))
        for hp in range(H_NA // 2):
            jobs.append((t, hp, start, var))
    scores = []
    for t, hp, start, var in jobs:
        cols = slice(hp * LANES, (hp + 1) * LANES)
        q = q_ref[0, t * tq:(t + 1) * tq, cols]
        zero = jnp.zeros_like(q)
        qs = jnp.concatenate([jnp.where(lo, q, zero), jnp.where(lo, zero, q)], axis=0)
        kw = k_ref[0, pl.ds(start, win), cols]
        bias = jnp.concatenate([bias_ref[var, 2 * hp], bias_ref[var, 2 * hp + 1]], axis=0)
        scores.append(lax.dot_general(qs, kw, (((1,), (1,)), ((), ())),
                                      preferred_element_type=F32) + bias)
    probs = [jnp.exp2(s - jnp.max(s, axis=1, keepdims=True)).astype(BF) for s in scores]
    for (t, hp, start, var), p in zip(jobs, probs):
        outs = []
        for hh, vref in enumerate((va, vb)):
            a = jnp.dot(p[hh * tq:(hh + 1) * tq], vref[hp, pl.ds(start, win), :],
                        preferred_element_type=F32)
            outs.append(a * (1.0 / pltpu.roll(a, HEAD_DIM, axis=1)))
        o_ref[0, t * tq:(t + 1) * tq, hp * LANES:(hp + 1) * LANES] = (
            jnp.where(lo, outs[0], outs[1]).astype(BF))


def _nbr_attention(qa, ka, va, bias2):
    B, S, W = qa.shape
    rows = S // GRID_W
    tq = TQ_BAND * NA_TILES_PER_STEP
    assert S % tq == 0
    bias_full = _tile_bias(bias2, rows)
    qblk = pl.BlockSpec((1, tq, W), lambda b, j: (b, j, 0))
    kblk = pl.BlockSpec((1, S, W), lambda b, j: (b, 0, 0))
    return pl.pallas_call(
        functools.partial(_nbr_body, rows=rows),
        grid=(B, S // tq),
        in_specs=[qblk, kblk, kblk, _resident(bias_full.shape)],
        out_specs=qblk,
        scratch_shapes=[pltpu.VMEM((H_NA // 2, S, LANES), BF) for _ in range(2)],
        out_shape=jax.ShapeDtypeStruct((B, S, W), BF),
        compiler_params=_cparams(2),
        name="nbr_attn",
    )(qa, ka, va, bias_full)


def _gqa_body(q_ref, kv_ref, o_ref, *, S):
    tq, tk = TQ_GQA, TK_GQA
    g = H_GQ // H_GKV
    lane = lax.broadcasted_iota(jnp.int32, (tq, LANES), 1)
    lo = lane < HEAD_DIM
    qs = []
    for kvh in range(H_GKV):
        mine = lo if kvh == 0 else jnp.logical_not(lo)
        parts = []
        for i in range(g):
            h = g * kvh + i
            chunk = q_ref[0, :, (h // 2) * LANES:(h // 2 + 1) * LANES]
            if (h % 2) != kvh:
                chunk = pltpu.roll(chunk.astype(F32), HEAD_DIM, axis=1).astype(BF)
            parts.append(jnp.where(mine, chunk, jnp.zeros_like(chunk)))
        qs.append(jnp.concatenate(parts, axis=0))

    def step(c, carry):
        rows = pl.ds(pl.multiple_of(c * tk, tk), tk)
        kblk = kv_ref[0, rows, 0:LANES]
        new = []
        for kvh in range(H_GKV):
            m, acc = carry[kvh]
            vblk = kv_ref[0, rows, (1 + kvh) * LANES:(2 + kvh) * LANES]
            s = lax.dot_general(qs[kvh], kblk, (((1,), (1,)), ((), ())),
                                preferred_element_type=F32)
            m_new = jnp.maximum(m, jnp.max(s, axis=1, keepdims=True))
            p = jnp.exp2(s - m_new)
            acc = jnp.exp2(m - m_new) * acc + jnp.dot(p.astype(BF), vblk,
                                                      preferred_element_type=F32)
            new.append((m_new, acc))
        return tuple(new)

    init = tuple((jnp.full((g * tq, 1), NEG_INF, F32), jnp.zeros((g * tq, LANES), F32))
                 for _ in range(H_GKV))
    fin = lax.fori_loop(0, S // tk, step, init, unroll=True)
    res = [acc * (1.0 / pltpu.roll(acc, HEAD_DIM, axis=1)) for _, acc in fin]
    for c in range(GQ_W // LANES):
        halves = []
        for hh in range(2):
            h = 2 * c + hh
            kvh, i = h // g, h % g
            v = res[kvh][i * tq:(i + 1) * tq]
            if kvh != hh:
                v = pltpu.roll(v, HEAD_DIM, axis=1)
            halves.append(v)
        o_ref[0, :, c * LANES:(c + 1) * LANES] = jnp.where(lo, halves[0], halves[1]).astype(BF)


def _gqa_attention(qc, kvc):
    B, S, W = qc.shape
    qblk = pl.BlockSpec((1, TQ_GQA, W), lambda b, j: (b, j, 0))
    kvblk = pl.BlockSpec((1, S, kvc.shape[-1]), lambda b, j: (b, 0, 0))
    return pl.pallas_call(
        functools.partial(_gqa_body, S=S),
        grid=(B, S // TQ_GQA),
        in_specs=[qblk, kvblk],
        out_specs=qblk,
        out_shape=jax.ShapeDtypeStruct((B, S, W), BF),
        compiler_params=_cparams(2),
        name="gqa_attn",
    )(qc, kvc)


def _ffn_body(*refs, tiles_per_seq, d_ff):
    (oa, oa_p, oa_n, od, od_p, od_n, oc, oc_p, oc_n, x, x_p, x_n,
     og_ref, wo_ref, g2_ref, wgu_ref, cw_ref, cb_ref, wd_ref, o_ref, hext, gs, act) = refs
    tm, halo = TM_FFN, FFN_HALO
    i = pl.program_id(0) % tiles_per_seq

    def ext(main, prev, nxt):
        return jnp.concatenate([prev[...], main[...], nxt[...]], axis=0)

    na = _rms(ext(oa, oa_p, oa_n).astype(F32), og_ref[:, :NA_W])
    nd = _rms(ext(od, od_p, od_n).astype(F32), og_ref[:, NA_W:NA_W + DIL_W])
    nc = _rms(ext(oc, oc_p, oc_n).astype(F32), og_ref[:, NA_W + DIL_W:])
    mix = jnp.concatenate([na, nd, nc], axis=1).astype(BF)
    y_ext = ext(x, x_p, x_n) + jnp.dot(mix, wo_ref[...], preferred_element_type=F32)
    y = y_ext[halo:halo + tm]
    row = lax.broadcasted_iota(jnp.int32, (tm + 2 * halo, 1), 0)
    keep = ((row >= halo) | (i > 0)) & ((row < halo + tm) | (i < tiles_per_seq - 1))
    hext[...] = jnp.where(keep, _rms(y_ext, g2_ref[...]), 0.0).astype(BF)
    for c in range(d_ff // MXU_N):
        cols = slice(c * MXU_N, (c + 1) * MXU_N)
        ucols = slice(d_ff + c * MXU_N, d_ff + (c + 1) * MXU_N)
        gs[...] = jnp.dot(hext[...], wgu_ref[:, cols], preferred_element_type=F32)
        u = jnp.dot(hext[halo:halo + tm], wgu_ref[:, ucols], preferred_element_type=F32)
        gc = gs[halo - 1:halo - 1 + tm] * cw_ref[0:1, cols] + cb_ref[:, cols]
        gc = gc + gs[halo:halo + tm] * cw_ref[1:2, cols]
        gc = gc + gs[halo + 1:halo + 1 + tm] * cw_ref[2:3, cols]
        gelu = 0.5 * gc * (1.0 + lax.erf(gc * (0.5 ** 0.5)))
        act[:, cols] = (gelu * u).astype(BF)
    o_ref[...] = y + jnp.dot(act[...], wd_ref[...], preferred_element_type=F32)


def _ffn(oa, od, oc, x2, og, w_out, g2, wgu, cw, cb, wd, l, S):
    N, D = x2.shape
    d_ff = wd.shape[1]
    tm, halo = TM_FFN, FFN_HALO
    assert S % tm == 0 and d_ff % MXU_N == 0
    r = tm // halo
    nblk = N // halo
    row = lambda i: (i, 0)
    prev = lambda i: (jnp.maximum(i * r - 1, 0), 0)
    nxt = lambda i: (jnp.minimum((i + 1) * r, nblk - 1), 0)
    tiled, args = [], []
    for a in (oa, od, oc, x2):
        w = a.shape[1]
        tiled += [pl.BlockSpec((tm, w), row), pl.BlockSpec((halo, w), prev),
                  pl.BlockSpec((halo, w), nxt)]
        args += [a, a, a]
    params = (og, w_out, g2, wgu, cw, cb, wd)
    return pl.pallas_call(
        functools.partial(_ffn_body, tiles_per_seq=S // tm, d_ff=d_ff),
        grid=(N // tm,),
        in_specs=tiled + [_resident_layer(p.shape, l) if p.ndim == 3 else _resident(p.shape)
                          for p in params],
        out_specs=pl.BlockSpec((tm, D), row),
        out_shape=jax.ShapeDtypeStruct((N, D), F32),
        scratch_shapes=[
            pltpu.VMEM((tm + 2 * halo, D), BF),
            pltpu.VMEM((tm + 2 * halo, MXU_N), F32),
            pltpu.VMEM((tm, d_ff), BF),
        ],
        compiler_params=_cparams(1),
        name="ffn",
    )(*args, *params)


def _rope_cos_sin(pos, dim, theta):
    inv = theta ** (-jnp.arange(0, dim, 2, dtype=F32) / dim)
    ang = inv[:, None] * pos.astype(F32)[None, :]
    cos, sin = lax.optimization_barrier((jnp.cos(ang), jnp.sin(ang)))
    return cos.T, sin.T


def _rotary_tables(S):
    t = jnp.arange(S, dtype=jnp.int32)
    cos1, sin1 = _rope_cos_sin(t, ROPE_DIMS, ROPE_THETA)
    rest = HEAD_DIM - ROPE_DIMS
    cb = jnp.concatenate([cos1, cos1, jnp.ones((S, rest), F32)], axis=1)
    sb = jnp.concatenate([-sin1, sin1, jnp.zeros((S, rest), F32)], axis=1)
    cg, sg = _rope_cos_sin(jnp.arange(GRID_W, dtype=jnp.int32), HEAD_DIM // 2, AXIAL_THETA)
    cr, sr = (jnp.repeat(a, S // GRID_W, axis=0) for a in (cg, sg))
    cc, sc = (jnp.tile(a, (S // GRID_W, 1)) for a in (cg, sg))
    ca = jnp.concatenate([cr, cr, cc, cc], axis=1)
    sa = jnp.concatenate([-sr, sr, -sc, sc], axis=1)
    two = lambda a: jnp.concatenate([a, a], axis=1)
    return two(cb), two(sb), two(ca), two(sa)


def _permute_in_columns(w):
    sizes = (NA_W, NA_W, NA_W, DIL_W, DIL_W, DIL_W, GQ_W, GKV_W, GKV_W)
    offs = [0]
    for s in sizes:
        offs.append(offs[-1] + s)
    seg = [w[..., offs[i]:offs[i + 1]] for i in range(len(sizes))]
    qa, ka, va, qd, kd, vd, qc, kc, vc = seg
    return jnp.concatenate([qa, ka, qd, kd, qc, kc, va, vd, vc], axis=-1)


def _gain_vector(qg, kg):
    scale = HEAD_DIM ** -0.5
    parts = [
        jnp.tile(qg[0] * (scale * LOG2E), H_NA), jnp.tile(kg[0], H_NA),
        jnp.tile(qg[1] * (scale * LOG2E), H_DIL), jnp.tile(kg[1], H_DIL),
        jnp.tile(qg[2] * (scale * LOG2E), H_GQ), jnp.tile(kg[2], H_GKV),
    ]
    return jnp.concatenate(parts)[None, :]


def kernel(x, norm1_g, w_in, q_norm_g, k_norm_g, rpb, out_norm_g, w_out, norm2_g,
           w_gate_up, conv_w, conv_b, w_down):
    B, S, D = x.shape
    depth = w_in.shape[0]
    N = B * S
    assert S == GRID_W * GRID_W and S % TM_PROJ == 0 and S % TQ_GQA == 0 and S % TK_GQA == 0
    tabs = _rotary_tables(S)
    idx = jnp.arange(MXU_N) // HEAD_DIM
    gsum = (idx[:, None] == idx[None, :]).astype(BF)
    x2 = x.reshape(N, D)
    w_perm = _permute_in_columns(w_in.astype(BF))
    w_out, w_gate_up, w_down = (w.astype(BF) for w in (w_out, w_gate_up, w_down))
    for l in range(depth):
        gain = _gain_vector(q_norm_g[l], k_norm_g[l])
        qa, ka, va, qd, kd, vd, qc, kvc = _inproj(
            x2, norm1_g[l][None, :], w_perm, l, gsum, gain, tabs, S)
        seq = lambda a: a.reshape(B, S, a.shape[-1])
        out_a = _nbr_attention(seq(qa), seq(ka), seq(va), _bias2(rpb[l]))
        out_d = _dilated_attention(seq(qd), seq(kd), seq(vd))
        out_c = _gqa_attention(seq(qc), seq(kvc))
        x2 = _ffn(out_a.reshape(N, NA_W), out_d.reshape(N, DIL_W), out_c.reshape(N, GQ_W), x2,
                  out_norm_g[l][None, :], w_out, norm2_g[l][None, :],
                  w_gate_up, conv_w[l], conv_b[l][None, :], w_down, l, S)
    return x2.reshape(B, S, D)
```

```python
import functools

import jax
import jax.numpy as jnp
from jax import lax
from jax.experimental import pallas as pl
from jax.experimental.pallas import tpu as pltpu

BF = jnp.bfloat16
F32 = jnp.float32

HEAD_DIM = 64
H_NA, H_DIL, H_GQ, H_GKV = 4, 6, 6, 2
NA_W, DIL_W, GQ_W, GKV_W = 256, 384, 384, 128
GRID_W = 64
NA_KH = 8
NA_KW = 16
DIL_PATTERNS = ((128, 1), (512, 4), (2048, 16))
BAND_HALF = 64
ROPE_THETA = 500000.0
ROPE_DIMS = 16
AXIAL_THETA = 10000.0
EPS = 1e-6
NEG_INF = -1e30
LOG2E = 1.4426950408889634

LANES = 128
MXU_N = 256
VMEM_LIMIT = 56 * 1024 * 1024

TM_IN = 1024
PROJ_SUB = 256
TM_FFN = 512
FFN_HALO = 16
TQ_BAND = 128
BAND_WIN = 256
TQ_GQA = 256
TK_GQA = 2048


def _cparams(n_axes):
    return pltpu.CompilerParams(
        dimension_semantics=("arbitrary",) * n_axes, vmem_limit_bytes=VMEM_LIMIT)


def _rms(v, g):
    ms = jnp.mean(v * v, axis=-1, keepdims=True)
    return v * lax.rsqrt(ms + EPS) * g


def _resident(shape):
    nd = len(shape)
    return pl.BlockSpec(shape, lambda *_: (0,) * nd, pipeline_mode=pl.Buffered(1))


def _resident_layer(stacked_shape, l):
    nd = len(stacked_shape)
    return pl.BlockSpec((None,) + tuple(stacked_shape[1:]), lambda *_: (l,) + (0,) * (nd - 1),
                        pipeline_mode=pl.Buffered(1))


def _rotary(p, cos, sin, first, half):
    up = pltpu.roll(p, LANES - half, axis=1)
    dn = pltpu.roll(p, half, axis=1)
    return p * cos + jnp.where(first, up, dn) * sin


def _inproj_body(x_ref, g1_ref, w_ref, gs_ref, gain_ref, cb_ref, sb_ref, cc_ref, sc_ref,
                 qa_ref, ka_ref, va_ref, qd_ref, kd_ref, vd_ref, qc_ref, kvc_ref):
    lane = lax.broadcasted_iota(jnp.int32, (1, LANES), 1)
    first_b = (lane % HEAD_DIM) < (ROPE_DIMS // 2)
    first_c = (lane % (HEAD_DIM // 2)) < (HEAD_DIM // 4)
    lo_half = lane < HEAD_DIM
    dests = (
        ((qa_ref, 0), (qa_ref, 128)),
        ((ka_ref, 0), (ka_ref, 128)),
        ((qd_ref, 0), (qd_ref, 128)),
        ((qd_ref, 256), (kd_ref, 0)),
        ((kd_ref, 128), (kd_ref, 256)),
        ((qc_ref, 0), (qc_ref, 128)),
        ((qc_ref, 256), (kvc_ref, 0)),
        ((va_ref, 0), (va_ref, 128)),
        ((vd_ref, 0), (vd_ref, 128)),
        ((vd_ref, 256), (kvc_ref, 128)),
    )
    for r0 in range(0, x_ref.shape[0], PROJ_SUB):
        rows = slice(r0, r0 + PROJ_SUB)
        h = _rms(x_ref[rows, :], g1_ref[...]).astype(BF)
        n_norm = gain_ref.shape[1]
        pq = jnp.dot(h, w_ref[:, :n_norm], preferred_element_type=F32)
        pv = jnp.dot(h, w_ref[:, n_norm:], preferred_element_type=F32)
        sq = (pq * pq).astype(BF)
        for c in range(10):
            cols = slice(c * MXU_N, (c + 1) * MXU_N)
            if c < 7:
                ss = jnp.dot(sq[:, cols], gs_ref[...], preferred_element_type=F32)
                p = pq[:, cols] * lax.rsqrt(ss * (1.0 / HEAD_DIM) + EPS) * gain_ref[:, cols]
            else:
                p = pv[:, (c - 7) * MXU_N:(c - 6) * MXU_N]
            halves = [p[:, :LANES], p[:, LANES:]]
            if 2 <= c <= 4:
                halves = [_rotary(v, cb_ref[rows, :], sb_ref[rows, :], first_b, ROPE_DIMS // 2)
                          for v in halves]
            elif 5 <= c <= 6:
                halves = [_rotary(v, cc_ref[rows, :], sc_ref[rows, :], first_c, HEAD_DIM // 4)
                          for v in halves]
            for v, (ref, off) in zip(halves, dests[c]):
                if ref is kvc_ref and off == LANES:
                    ref[rows, LANES:2 * LANES] = jnp.where(lo_half, v, 1.0).astype(BF)
                    ref[rows, 2 * LANES:3 * LANES] = jnp.where(lo_half, 1.0, v).astype(BF)
                else:
                    ref[rows, off:off + LANES] = v.astype(BF)


def _inproj(x2, g1, w_perm, l, gsum, gain, tabs, S):
    N, D = x2.shape
    tm = TM_IN
    assert S % tm == 0 and tm % PROJ_SUB == 0
    nt_seq = S // tm
    row = lambda i: (i, 0)
    tab = pl.BlockSpec((tm, LANES), lambda i: (i % nt_seq, 0))
    widths = (NA_W, NA_W, NA_W, DIL_W, DIL_W, DIL_W, GQ_W, 3 * GKV_W)
    return pl.pallas_call(
        _inproj_body,
        grid=(N // tm,),
        in_specs=[
            pl.BlockSpec((tm, D), row),
            _resident(g1.shape),
            _resident_layer(w_perm.shape, l),
            _resident(gsum.shape),
            _resident(gain.shape),
            tab, tab, tab, tab,
        ],
        out_specs=[pl.BlockSpec((tm, w), row) for w in widths],
        out_shape=[jax.ShapeDtypeStruct((N, w), BF) for w in widths],
        compiler_params=_cparams(1),
        name="inproj",
    )(x2, g1, w_perm, gsum, gain, *tabs)


DIL_STRIDES = tuple(d for _, d in DIL_PATTERNS)


def _band_bias():
    r = jnp.arange(TQ_BAND)[:, None]
    c = jnp.arange(BAND_WIN)[None, :]
    masks = [jnp.where(jnp.abs(r + off - c) <= BAND_HALF, 0.0, NEG_INF).astype(F32)
             for off in (0, BAND_HALF, 2 * BAND_HALF)]
    return jnp.stack(masks)


def _dil_body(q_ref, k_ref, v_ref, bias_ref, o_ref,
              stage, stage4, kc, vac, vbc, qc, acc, mx, *, S):
    tq, win = TQ_BAND, BAND_WIN
    half = S // 2
    hf = pl.program_id(2)
    lo = lax.broadcasted_iota(jnp.int32, (tq, LANES), 1) < HEAD_DIM

    def deinterleave(dsts, g0, rows):
        n4, n16 = rows // 4, rows // 16
        for r in range(4):
            cls = stage[pl.ds(r, n4, stride=4), :]
            stage4[r * n4:(r + 1) * n4, :] = cls
            for ref, fn in dsts:
                ref[g0, r * n4:(r + 1) * n4, :] = fn(cls).astype(BF)
        for r in range(16):
            cls = stage4[pl.ds((r % 4) * n4 + r // 4, n16, stride=4), :]
            for ref, fn in dsts:
                ref[g0 + 1, r * n16:(r + 1) * n16, :] = fn(cls).astype(BF)

    def lo_of(a):
        return lax.broadcasted_iota(jnp.int32, a.shape, 1) < HEAD_DIM

    same = lambda a: a
    ones_right = lambda a: jnp.where(lo_of(a), a, 1.0)
    ones_left = lambda a: jnp.where(lo_of(a), 1.0, a)

    @pl.when(hf == 0)
    def _():
        stage[...] = k_ref[0].astype(F32)
        deinterleave([(kc, same)], 0, S)
        v = v_ref[0]
        stage[...] = v.astype(F32)
        vac[0] = ones_right(v).astype(BF)
        vbc[0] = ones_left(v).astype(BF)
        deinterleave([(vac, ones_right), (vbc, ones_left)], 1, S)

    stage[0:half] = q_ref[0].astype(F32)
    deinterleave([(qc, same)], 0, half)

    for g, d in enumerate(DIL_STRIDES):
        L = S // d
        n = half // d
        for r in range(d):
            for jj in range(n // tq):
                i0 = hf * n + jj * tq
                start = jnp.clip(i0 - BAND_HALF, 0, L - win)
                kv_rows = pl.ds(pl.multiple_of(r * L + start, BAND_HALF), win)
                if g == 0:
                    q = q_ref[0, jj * tq:(jj + 1) * tq, :]
                    kwin = k_ref[0, kv_rows, :]
                    rows = pl.ds(jj * tq, tq)
                else:
                    q = qc[g - 1, r * n + jj * tq:r * n + (jj + 1) * tq, :]
                    kwin = kc[g - 1, kv_rows, :]
                    rows = pl.ds(jj * tq * d + r, tq, stride=d)
                bias = bias_ref[(i0 - start) // BAND_HALF]
                zero = jnp.zeros_like(q)
                for hd, vref in enumerate((vac, vbc)):
                    qm = jnp.where(lo, q, zero) if hd == 0 else jnp.where(lo, zero, q)
                    s = lax.dot_general(qm, kwin, (((1,), (1,)), ((), ())),
                                        preferred_element_type=F32) + bias
                    m = jnp.max(s, axis=1, keepdims=True)
                    p = jnp.exp2(s - m).astype(BF)
                    acc[g, hd, rows, :] = jnp.dot(p, vref[g, kv_rows, :],
                                                  preferred_element_type=F32)
                    mx[g, hd, rows, :] = jnp.broadcast_to(m, (tq, LANES))

    chunk = 2 * tq
    lo_c = lax.broadcasted_iota(jnp.int32, (chunk, LANES), 1) < HEAD_DIM
    n_pat = len(DIL_STRIDES)
    for c in range(half // chunk):
        rows = slice(c * chunk, (c + 1) * chunk)
        outs = []
        for hd in range(2):
            m = [mx[g, hd, rows, :] for g in range(n_pat)]
            m_all = functools.reduce(jnp.maximum, m)
            tot = sum(jnp.exp2(m[g] - m_all) * acc[g, hd, rows, :] for g in range(n_pat))
            outs.append(tot * (1.0 / pltpu.roll(tot, HEAD_DIM, axis=1)))
        o_ref[0, rows, :] = jnp.where(lo_c, outs[0], outs[1]).astype(BF)


def _dilated_attention(qd, kd, vd):
    B, S, W = qd.shape
    assert all((w // 2) // d == BAND_HALF for w, d in DIL_PATTERNS)
    assert DIL_STRIDES == (1, 4, 16) and S // DIL_STRIDES[-1] == BAND_WIN
    assert (S // 2) % (TQ_BAND * DIL_STRIDES[-1]) == 0
    half = S // 2
    bias = _band_bias()
    n_pat = len(DIL_STRIDES)
    return pl.pallas_call(
        functools.partial(_dil_body, S=S),
        grid=(B, W // LANES, 2),
        in_specs=[
            pl.BlockSpec((1, half, LANES), lambda b, p, h: (b, h, p)),
            pl.BlockSpec((1, S, LANES), lambda b, p, h: (b, 0, p)),
            pl.BlockSpec((1, S, LANES), lambda b, p, h: (b, 0, p)),
            _resident(bias.shape),
        ],
        out_specs=pl.BlockSpec((1, half, LANES), lambda b, p, h: (b, h, p)),
        out_shape=jax.ShapeDtypeStruct((B, S, W), BF),
        scratch_shapes=[
            pltpu.VMEM((S, LANES), F32),
            pltpu.VMEM((S, LANES), F32),
            pltpu.VMEM((n_pat - 1, S, LANES), BF),
            pltpu.VMEM((n_pat, S, LANES), BF),
            pltpu.VMEM((n_pat, S, LANES), BF),
            pltpu.VMEM((n_pat - 1, half, LANES), BF),
            pltpu.VMEM((n_pat, 2, half, LANES), F32),
            pltpu.VMEM((n_pat, 2, half, LANES), F32),
        ],
        compiler_params=_cparams(3),
        name="dilated_attn",
    )(qd, kd, vd, bias)


NA_ROWS_WIN = 10
NA_DR = 2 * NA_KH - 1
NA_DC = 2 * NA_KW - 1


def _bias2_body(rpb_ref, o_ref):
    cq = lax.broadcasted_iota(jnp.int32, (GRID_W, LANES), 0)
    lane = lax.broadcasted_iota(jnp.int32, (GRID_W, LANES), 1)
    kc = lane % GRID_W
    hi = lane >= GRID_W
    c0 = jnp.clip(cq - NA_KW // 2, 0, GRID_W - NA_KW)
    col_ok = (kc >= c0) & (kc < c0 + NA_KW)
    for h in range(H_NA):
        for d in range(NA_DR + 1):
            rows = [jnp.broadcast_to(rpb_ref[h, d + k:d + k + 1, :], (GRID_W, LANES)) for k in (0, 1)]
            a = pltpu.roll(rows[0], LANES - (NA_KW - 1), axis=1, stride=1, stride_axis=0)
            b = pltpu.roll(rows[1], GRID_W - (NA_KW - 1), axis=1, stride=1, stride_axis=0)
            ok = col_ok
            if d == 0:
                ok = col_ok & hi
            elif d == NA_DR:
                ok = col_ok & jnp.logical_not(hi)
            o_ref[h, d] = jnp.where(ok, jnp.where(hi, b, a) * LOG2E, NEG_INF)


def _bias2(rpb_l):
    h, ndr, ndc = rpb_l.shape
    rpb_pad = jnp.pad(rpb_l, ((0, 0), (1, 1), (0, LANES - ndc)))
    shape = (H_NA, NA_DR + 1, GRID_W, LANES)
    return pl.pallas_call(
        _bias2_body,
        grid=(1,),
        in_specs=[_resident(rpb_pad.shape)],
        out_specs=pl.BlockSpec(shape, lambda i: (0, 0, 0, 0)),
        out_shape=jax.ShapeDtypeStruct(shape, F32),
        compiler_params=_cparams(1),
        name="nbr_bias",
    )(rpb_pad)


NA_TILE_VARIANTS = 5
NA_TILES_PER_STEP = 8


def _nbr_window_row(j, rows):
    return jnp.clip(2 * j - NA_KH // 2, 0, rows - NA_ROWS_WIN)


def _tile_bias_body(b2_ref, o_ref, *, rows):
    v = pl.program_id(0)
    n_tiles = rows // 2
    j = jnp.where(v < 2, v, jnp.where(v == 2, 2, v + (n_tiles - NA_TILE_VARIANTS)))
    rs = _nbr_window_row(j, rows)
    lane_b = lax.broadcasted_iota(jnp.int32, (GRID_W, LANES), 1) // GRID_W
    for h in range(H_NA):
        bias_rows = []
        for qi in range(2):
            rq = 2 * j + qi
            r0 = jnp.clip(rq - NA_KH // 2, 0, rows - NA_KH)
            blocks = []
            for kk in range(NA_ROWS_WIN // 2):
                kr = rs + 2 * kk
                d = jnp.clip(kr - rq + NA_KH, 0, NA_DR)
                krl = kr + lane_b
                ok = (krl >= r0) & (krl < r0 + NA_KH)
                blocks.append(jnp.where(ok, b2_ref[h, d], NEG_INF))
            bias_rows.append(jnp.concatenate(blocks, axis=1))
        o_ref[0, h] = jnp.concatenate(bias_rows, axis=0)


def _tile_bias(bias2, rows):
    assert rows // 2 > NA_TILE_VARIANTS
    win = NA_ROWS_WIN * GRID_W
    return pl.pallas_call(
        functools.partial(_tile_bias_body, rows=rows),
        grid=(NA_TILE_VARIANTS,),
        in_specs=[_resident(bias2.shape)],
        out_specs=pl.BlockSpec((1, H_NA, TQ_BAND, win), lambda v: (v, 0, 0, 0)),
        out_shape=jax.ShapeDtypeStruct((NA_TILE_VARIANTS, H_NA, TQ_BAND, win), F32),
        compiler_params=_cparams(1),
        name="nbr_tile_bias",
    )(bias2)


def _nbr_body(q_ref, k_ref, v_ref, bias_ref, o_ref, va, vb, *, rows):
    tq = TQ_BAND
    win = NA_ROWS_WIN * GRID_W
    n_tiles = rows // 2
    S = rows * GRID_W
    jt = pl.program_id(1)

    @pl.when(jt == 0)
    def _():
        lo_s = lax.broadcasted_iota(jnp.int32, (S, LANES), 1) < HEAD_DIM
        for hp in range(H_NA // 2):
            v = v_ref[0, :, hp * LANES:(hp + 1) * LANES]
            va[hp] = jnp.where(lo_s, v, jnp.ones_like(v))
            vb[hp] = jnp.where(lo_s, jnp.ones_like(v), v)

    lo = lax.broadcasted_iota(jnp.int32, (tq, LANES), 1) < HEAD_DIM
    jobs = []
    for t in range(NA_TILES_PER_STEP):
        j = jt * NA_TILES_PER_STEP + t
        start = pl.multiple_of(_nbr_window_row(j, rows) * GRID_W, 2 * GRID_W)
        var = jnp.where(j < 2, j, jnp.where(j >= n_tiles - 2, j - (n_tiles - NA_TILE_VARIANTS), 2))
        for hp in range(H_NA // 2):
            jobs.append((t, hp, start, var))
    scores = []
    for t, hp, start, var in jobs:
        cols = slice(hp * LANES, (hp + 1) * LANES)
        q = q_ref[0, t * tq:(t + 1) * tq, cols]
        zero = jnp.zeros_like(q)
        qs = jnp.concatenate([jnp.where(lo, q, zero), jnp.where(lo, zero, q)], axis=0)
        kw = k_ref[0, pl.ds(start, win), cols]
        bias = jnp.concatenate([bias_ref[var, 2 * hp], bias_ref[var, 2 * hp + 1]], axis=0)
        scores.append(lax.dot_general(qs, kw, (((1,), (1,)), ((), ())),
                                      preferred_element_type=F32) + bias)
    probs = [jnp.exp2(s - jnp.max(s, axis=1, keepdims=True)).astype(BF) for s in scores]
    for (t, hp, start, var), p in zip(jobs, probs):
        outs = []
        for hh, vref in enumerate((va, vb)):
            a = jnp.dot(p[hh * tq:(hh + 1) * tq], vref[hp, pl.ds(start, win), :],
                        preferred_element_type=F32)
            outs.append(a * (1.0 / pltpu.roll(a, HEAD_DIM, axis=1)))
        o_ref[0, t * tq:(t + 1) * tq, hp * LANES:(hp + 1) * LANES] = (
            jnp.where(lo, outs[0], outs[1]).astype(BF))


def _nbr_attention(qa, ka, va, bias2):
    B, S, W = qa.shape
    rows = S // GRID_W
    tq = TQ_BAND * NA_TILES_PER_STEP
    assert S % tq == 0
    bias_full = _tile_bias(bias2, rows)
    qblk = pl.BlockSpec((1, tq, W), lambda b, j: (b, j, 0))
    kblk = pl.BlockSpec((1, S, W), lambda b, j: (b, 0, 0))
    return pl.pallas_call(
        functools.partial(_nbr_body, rows=rows),
        grid=(B, S // tq),
        in_specs=[qblk, kblk, kblk, _resident(bias_full.shape)],
        out_specs=qblk,
        scratch_shapes=[pltpu.VMEM((H_NA // 2, S, LANES), BF) for _ in range(2)],
        out_shape=jax.ShapeDtypeStruct((B, S, W), BF),
        compiler_params=_cparams(2),
        name="nbr_attn",
    )(qa, ka, va, bias_full)


def _gqa_body(q_ref, kv_ref, o_ref, *, S):
    tq, tk = TQ_GQA, TK_GQA
    g = H_GQ // H_GKV
    lane = lax.broadcasted_iota(jnp.int32, (tq, LANES), 1)
    lo = lane < HEAD_DIM
    qs = []
    for kvh in range(H_GKV):
        mine = lo if kvh == 0 else jnp.logical_not(lo)
        parts = []
        for i in range(g):
            h = g * kvh + i
            chunk = q_ref[0, :, (h // 2) * LANES:(h // 2 + 1) * LANES]
            if (h % 2) != kvh:
                chunk = pltpu.roll(chunk.astype(F32), HEAD_DIM, axis=1).astype(BF)
            parts.append(jnp.where(mine, chunk, jnp.zeros_like(chunk)))
        qs.append(jnp.concatenate(parts, axis=0))

    def step(c, carry):
        rows = pl.ds(pl.multiple_of(c * tk, tk), tk)
        kblk = kv_ref[0, rows, 0:LANES]
        new = []
        for kvh in range(H_GKV):
            m, acc = carry[kvh]
            vblk = kv_ref[0, rows, (1 + kvh) * LANES:(2 + kvh) * LANES]
            s = lax.dot_general(qs[kvh], kblk, (((1,), (1,)), ((), ())),
                                preferred_element_type=F32)
            m_new = jnp.maximum(m, jnp.max(s, axis=1, keepdims=True))
            p = jnp.exp2(s - m_new)
            acc = jnp.exp2(m - m_new) * acc + jnp.dot(p.astype(BF), vblk,
                                                      preferred_element_type=F32)
            new.append((m_new, acc))
        return tuple(new)

    init = tuple((jnp.full((g * tq, 1), NEG_INF, F32), jnp.zeros((g * tq, LANES), F32))
                 for _ in range(H_GKV))
    fin = lax.fori_loop(0, S // tk, step, init, unroll=True)
    res = [acc * (1.0 / pltpu.roll(acc, HEAD_DIM, axis=1)) for _, acc in fin]
    for c in range(GQ_W // LANES):
        halves = []
        for hh in range(2):
            h = 2 * c + hh
            kvh, i = h // g, h % g
            v = res[kvh][i * tq:(i + 1) * tq]
            if kvh != hh:
                v = pltpu.roll(v, HEAD_DIM, axis=1)
            halves.append(v)
        o_ref[0, :, c * LANES:(c + 1) * LANES] = jnp.where(lo, halves[0], halves[1]).astype(BF)


def _gqa_attention(qc, kvc):
    B, S, W = qc.shape
    qblk = pl.BlockSpec((1, TQ_GQA, W), lambda b, j: (b, j, 0))
    kvblk = pl.BlockSpec((1, S, kvc.shape[-1]), lambda b, j: (b, 0, 0))
    return pl.pallas_call(
        functools.partial(_gqa_body, S=S),
        grid=(B, S // TQ_GQA),
        in_specs=[qblk, kvblk],
        out_specs=qblk,
        out_shape=jax.ShapeDtypeStruct((B, S, W), BF),
        compiler_params=_cparams(2),
        name="gqa_attn",
    )(qc, kvc)


def _ffn_body(*refs, tiles_per_seq, d_ff):
    (oa, oa_p, oa_n, od, od_p, od_n, oc, oc_p, oc_n, x, x_p, x_n,
     og_ref, wo_ref, g2_ref, wgu_ref, cw_ref, cb_ref, wd_ref, o_ref, hext, gs, act) = refs
    tm, halo = TM_FFN, FFN_HALO
    i = pl.program_id(0) % tiles_per_seq

    def ext(main, prev, nxt):
        return jnp.concatenate([prev[...], main[...], nxt[...]], axis=0)

    na = _rms(ext(oa, oa_p, oa_n).astype(F32), og_ref[:, :NA_W])
    nd = _rms(ext(od, od_p, od_n).astype(F32), og_ref[:, NA_W:NA_W + DIL_W])
    nc = _rms(ext(oc, oc_p, oc_n).astype(F32), og_ref[:, NA_W + DIL_W:])
    mix = jnp.concatenate([na, nd, nc], axis=1).astype(BF)
    y_ext = ext(x, x_p, x_n) + jnp.dot(mix, wo_ref[...], preferred_element_type=F32)
    y = y_ext[halo:halo + tm]
    row = lax.broadcasted_iota(jnp.int32, (tm + 2 * halo, 1), 0)
    keep = ((row >= halo) | (i > 0)) & ((row < halo + tm) | (i < tiles_per_seq - 1))
    hext[...] = jnp.where(keep, _rms(y_ext, g2_ref[...]), 0.0).astype(BF)
    for c in range(d_ff // MXU_N):
        cols = slice(c * MXU_N, (c + 1) * MXU_N)
        ucols = slice(d_ff + c * MXU_N, d_ff + (c + 1) * MXU_N)
        gs[...] = jnp.dot(hext[...], wgu_ref[:, cols], preferred_element_type=F32)
        u = jnp.dot(hext[halo:halo + tm], wgu_ref[:, ucols], preferred_element_type=F32)
        gc = gs[halo - 1:halo - 1 + tm] * cw_ref[0:1, cols] + cb_ref[:, cols]
        gc = gc + gs[halo:halo + tm] * cw_ref[1:2, cols]
        gc = gc + gs[halo + 1:halo + 1 + tm] * cw_ref[2:3, cols]
        gelu = 0.5 * gc * (1.0 + lax.erf(gc * (0.5 ** 0.5)))
        act[:, cols] = (gelu * u).astype(BF)
    o_ref[...] = y + jnp.dot(act[...], wd_ref[...], preferred_element_type=F32)


def _ffn(oa, od, oc, x2, og, w_out, g2, wgu, cw, cb, wd, l, S):
    N, D = x2.shape
    d_ff = wd.shape[1]
    tm, halo = TM_FFN, FFN_HALO
    assert S % tm == 0 and d_ff % MXU_N == 0
    r = tm // halo
    nblk = N // halo
    row = lambda i: (i, 0)
    prev = lambda i: (jnp.maximum(i * r - 1, 0), 0)
    nxt = lambda i: (jnp.minimum((i + 1) * r, nblk - 1), 0)
    tiled, args = [], []
    for a in (oa, od, oc, x2):
        w = a.shape[1]
        tiled += [pl.BlockSpec((tm, w), row), pl.BlockSpec((halo, w), prev),
                  pl.BlockSpec((halo, w), nxt)]
        args += [a, a, a]
    params = (og, w_out, g2, wgu, cw, cb, wd)
    return pl.pallas_call(
        functools.partial(_ffn_body, tiles_per_seq=S // tm, d_ff=d_ff),
        grid=(N // tm,),
        in_specs=tiled + [_resident_layer(p.shape, l) if p.ndim == 3 else _resident(p.shape)
                          for p in params],
        out_specs=pl.BlockSpec((tm, D), row),
        out_shape=jax.ShapeDtypeStruct((N, D), F32),
        scratch_shapes=[
            pltpu.VMEM((tm + 2 * halo, D), BF),
            pltpu.VMEM((tm + 2 * halo, MXU_N), F32),
            pltpu.VMEM((tm, d_ff), BF),
        ],
        compiler_params=_cparams(1),
        name="ffn",
    )(*args, *params)


def _rope_cos_sin(pos, dim, theta):
    inv = theta ** (-jnp.arange(0, dim, 2, dtype=F32) / dim)
    ang = inv[:, None] * pos.astype(F32)[None, :]
    cos, sin = lax.optimization_barrier((jnp.cos(ang), jnp.sin(ang)))
    return cos.T, sin.T


def _rotary_tables(S):
    t = jnp.arange(S, dtype=jnp.int32)
    cos1, sin1 = _rope_cos_sin(t, ROPE_DIMS, ROPE_THETA)
    rest = HEAD_DIM - ROPE_DIMS
    cb = jnp.concatenate([cos1, cos1, jnp.ones((S, rest), F32)], axis=1)
    sb = jnp.concatenate([-sin1, sin1, jnp.zeros((S, rest), F32)], axis=1)
    cg, sg = _rope_cos_sin(jnp.arange(GRID_W, dtype=jnp.int32), HEAD_DIM // 2, AXIAL_THETA)
    cr, sr = (jnp.repeat(a, S // GRID_W, axis=0) for a in (cg, sg))
    cc, sc = (jnp.tile(a, (S // GRID_W, 1)) for a in (cg, sg))
    ca = jnp.concatenate([cr, cr, cc, cc], axis=1)
    sa = jnp.concatenate([-sr, sr, -sc, sc], axis=1)
    two = lambda a: jnp.concatenate([a, a], axis=1)
    return two(cb), two(sb), two(ca), two(sa)


def _permute_in_columns(w):
    sizes = (NA_W, NA_W, NA_W, DIL_W, DIL_W, DIL_W, GQ_W, GKV_W, GKV_W)
    offs = [0]
    for s in sizes:
        offs.append(offs[-1] + s)
    seg = [w[..., offs[i]:offs[i + 1]] for i in range(len(sizes))]
    qa, ka, va, qd, kd, vd, qc, kc, vc = seg
    return jnp.concatenate([qa, ka, qd, kd, qc, kc, va, vd, vc], axis=-1)


def _gain_vector(qg, kg):
    scale = HEAD_DIM ** -0.5
    parts = [
        jnp.tile(qg[0] * (scale * LOG2E), H_NA), jnp.tile(kg[0], H_NA),
        jnp.tile(qg[1] * (scale * LOG2E), H_DIL), jnp.tile(kg[1], H_DIL),
        jnp.tile(qg[2] * (scale * LOG2E), H_GQ), jnp.tile(kg[2], H_GKV),
    ]
    return jnp.concatenate(parts)[None, :]


def kernel(x, norm1_g, w_in, q_norm_g, k_norm_g, rpb, out_norm_g, w_out, norm2_g,
           w_gate_up, conv_w, conv_b, w_down):
    B, S, D = x.shape
    depth = w_in.shape[0]
    N = B * S
    assert S == GRID_W * GRID_W and S % TQ_GQA == 0 and S % TK_GQA == 0
    tabs = _rotary_tables(S)
    idx = jnp.arange(MXU_N) // HEAD_DIM
    gsum = (idx[:, None] == idx[None, :]).astype(BF)
    x2 = x.reshape(N, D)
    w_perm = _permute_in_columns(w_in.astype(BF))
    w_out, w_gate_up, w_down = (w.astype(BF) for w in (w_out, w_gate_up, w_down))
    for l in range(depth):
        gain = _gain_vector(q_norm_g[l], k_norm_g[l])
        qa, ka, va, qd, kd, vd, qc, kvc = _inproj(
            x2, norm1_g[l][None, :], w_perm, l, gsum, gain, tabs, S)
        seq = lambda a: a.reshape(B, S, a.shape[-1])
        out_a = _nbr_attention(seq(qa), seq(ka), seq(va), _bias2(rpb[l]))
        out_d = _dilated_attention(seq(qd), seq(kd), seq(vd))
        out_c = _gqa_attention(seq(qc), seq(kvc))
        x2 = _ffn(out_a.reshape(N, NA_W), out_d.reshape(N, DIL_W), out_c.reshape(N, GQ_W), x2,
                  out_norm_g[l][None, :], w_out, norm2_g[l][None, :],
                  w_gate_up, conv_w[l], conv_b[l][None, :], w_down, l, S)
    return x2.reshape(B, S, D)
```

```python
import functools

import jax
import jax.numpy as jnp
from jax import lax
from jax.experimental import pallas as pl
from jax.experimental.pallas import tpu as pltpu

BF = jnp.bfloat16
F32 = jnp.float32

HEAD_DIM = 64
H_NA, H_DIL, H_GQ, H_GKV = 4, 6, 6, 2
NA_W, DIL_W, GQ_W, GKV_W = 256, 384, 384, 128
GRID_W = 64
NA_KH = 8
NA_KW = 16
DIL_PATTERNS = ((128, 1), (512, 4), (2048, 16))
BAND_HALF = 64
ROPE_THETA = 500000.0
ROPE_DIMS = 16
AXIAL_THETA = 10000.0
EPS = 1e-6
NEG_INF = -1e30
LOG2E = 1.4426950408889634

LANES = 128
MXU_N = 256
VMEM_LIMIT = 56 * 1024 * 1024

TM_IN = 1024
PROJ_SUB = 256
TM_FFN = 1024
FFN_HALO = 16
TQ_BAND = 128
BAND_WIN = 256
TQ_GQA = 256
TK_GQA = 2048


def _cparams(n_axes):
    return pltpu.CompilerParams(
        dimension_semantics=("arbitrary",) * n_axes, vmem_limit_bytes=VMEM_LIMIT)


def _rms(v, g):
    ms = jnp.mean(v * v, axis=-1, keepdims=True)
    return v * lax.rsqrt(ms + EPS) * g


def _resident(shape):
    nd = len(shape)
    return pl.BlockSpec(shape, lambda *_: (0,) * nd, pipeline_mode=pl.Buffered(1))


def _resident_layer(stacked_shape, l):
    nd = len(stacked_shape)
    return pl.BlockSpec((None,) + tuple(stacked_shape[1:]), lambda *_: (l,) + (0,) * (nd - 1),
                        pipeline_mode=pl.Buffered(1))


def _rotary(p, cos, sin, first, half):
    up = pltpu.roll(p, LANES - half, axis=1)
    dn = pltpu.roll(p, half, axis=1)
    return p * cos + jnp.where(first, up, dn) * sin


def _inproj_body(x_ref, g1_ref, w_ref, gs_ref, gain_ref, cb_ref, sb_ref, cc_ref, sc_ref,
                 qa_ref, ka_ref, va_ref, qd_ref, kd_ref, vd_ref, qc_ref, kvc_ref):
    lane = lax.broadcasted_iota(jnp.int32, (1, LANES), 1)
    first_b = (lane % HEAD_DIM) < (ROPE_DIMS // 2)
    first_c = (lane % (HEAD_DIM // 2)) < (HEAD_DIM // 4)
    lo_half = lane < HEAD_DIM
    dests = (
        ((qa_ref, 0), (qa_ref, 128)),
        ((ka_ref, 0), (ka_ref, 128)),
        ((qd_ref, 0), (qd_ref, 128)),
        ((qd_ref, 256), (kd_ref, 0)),
        ((kd_ref, 128), (kd_ref, 256)),
        ((qc_ref, 0), (qc_ref, 128)),
        ((qc_ref, 256), (kvc_ref, 0)),
        ((va_ref, 0), (va_ref, 128)),
        ((vd_ref, 0), (vd_ref, 128)),
        ((vd_ref, 256), (kvc_ref, 128)),
    )
    for r0 in range(0, x_ref.shape[0], PROJ_SUB):
        rows = slice(r0, r0 + PROJ_SUB)
        h = _rms(x_ref[rows, :], g1_ref[...]).astype(BF)
        n_norm = gain_ref.shape[1]
        pq = jnp.dot(h, w_ref[:, :n_norm], preferred_element_type=F32)
        pv = jnp.dot(h, w_ref[:, n_norm:], preferred_element_type=F32)
        sq = (pq * pq).astype(BF)
        for c in range(10):
            cols = slice(c * MXU_N, (c + 1) * MXU_N)
            if c < 7:
                ss = jnp.dot(sq[:, cols], gs_ref[...], preferred_element_type=F32)
                p = pq[:, cols] * lax.rsqrt(ss * (1.0 / HEAD_DIM) + EPS) * gain_ref[:, cols]
            else:
                p = pv[:, (c - 7) * MXU_N:(c - 6) * MXU_N]
            halves = [p[:, :LANES], p[:, LANES:]]
            if 2 <= c <= 4:
                halves = [_rotary(v, cb_ref[rows, :], sb_ref[rows, :], first_b, ROPE_DIMS // 2)
                          for v in halves]
            elif 5 <= c <= 6:
                halves = [_rotary(v, cc_ref[rows, :], sc_ref[rows, :], first_c, HEAD_DIM // 4)
                          for v in halves]
            for v, (ref, off) in zip(halves, dests[c]):
                if ref is kvc_ref and off == LANES:
                    ref[rows, LANES:2 * LANES] = jnp.where(lo_half, v, 1.0).astype(BF)
                    ref[rows, 2 * LANES:3 * LANES] = jnp.where(lo_half, 1.0, v).astype(BF)
                else:
                    ref[rows, off:off + LANES] = v.astype(BF)


def _inproj(x2, g1, w_perm, l, gsum, gain, tabs, S):
    N, D = x2.shape
    tm = TM_IN
    assert S % tm == 0 and tm % PROJ_SUB == 0
    nt_seq = S // tm
    row = lambda i: (i, 0)
    tab = pl.BlockSpec((tm, LANES), lambda i: (i % nt_seq, 0))
    widths = (NA_W, NA_W, NA_W, DIL_W, DIL_W, DIL_W, GQ_W, 3 * GKV_W)
    return pl.pallas_call(
        _inproj_body,
        grid=(N // tm,),
        in_specs=[
            pl.BlockSpec((tm, D), row),
            _resident(g1.shape),
            _resident_layer(w_perm.shape, l),
            _resident(gsum.shape),
            _resident(gain.shape),
            tab, tab, tab, tab,
        ],
        out_specs=[pl.BlockSpec((tm, w), row) for w in widths],
        out_shape=[jax.ShapeDtypeStruct((N, w), BF) for w in widths],
        compiler_params=_cparams(1),
        name="inproj",
    )(x2, g1, w_perm, gsum, gain, *tabs)


DIL_STRIDES = tuple(d for _, d in DIL_PATTERNS)


def _band_bias():
    r = jnp.arange(TQ_BAND)[:, None]
    c = jnp.arange(BAND_WIN)[None, :]
    masks = [jnp.where(jnp.abs(r + off - c) <= BAND_HALF, 0.0, NEG_INF).astype(F32)
             for off in (0, BAND_HALF, 2 * BAND_HALF)]
    return jnp.stack(masks)


def _dil_body(q_ref, k_ref, v_ref, bias_ref, o_ref,
              stage, stage4, kc, vac, vbc, qc, acc, mx, *, S):
    tq, win = TQ_BAND, BAND_WIN
    half = S // 2
    hf = pl.program_id(2)
    lo = lax.broadcasted_iota(jnp.int32, (tq, LANES), 1) < HEAD_DIM

    def deinterleave(dsts, g0, rows):
        n4, n16 = rows // 4, rows // 16
        for r in range(4):
            cls = stage[pl.ds(r, n4, stride=4), :]
            stage4[r * n4:(r + 1) * n4, :] = cls
            for ref, fn in dsts:
                ref[g0, r * n4:(r + 1) * n4, :] = fn(cls).astype(BF)
        for r in range(16):
            cls = stage4[pl.ds((r % 4) * n4 + r // 4, n16, stride=4), :]
            for ref, fn in dsts:
                ref[g0 + 1, r * n16:(r + 1) * n16, :] = fn(cls).astype(BF)

    def lo_of(a):
        return lax.broadcasted_iota(jnp.int32, a.shape, 1) < HEAD_DIM

    same = lambda a: a
    ones_right = lambda a: jnp.where(lo_of(a), a, 1.0)
    ones_left = lambda a: jnp.where(lo_of(a), 1.0, a)

    @pl.when(hf == 0)
    def _():
        stage[...] = k_ref[0].astype(F32)
        deinterleave([(kc, same)], 0, S)
        v = v_ref[0]
        stage[...] = v.astype(F32)
        vac[0] = ones_right(v).astype(BF)
        vbc[0] = ones_left(v).astype(BF)
        deinterleave([(vac, ones_right), (vbc, ones_left)], 1, S)

    stage[0:half] = q_ref[0].astype(F32)
    deinterleave([(qc, same)], 0, half)

    for g, d in enumerate(DIL_STRIDES):
        L = S // d
        n = half // d
        for r in range(d):
            for jj in range(n // tq):
                i0 = hf * n + jj * tq
                start = jnp.clip(i0 - BAND_HALF, 0, L - win)
                kv_rows = pl.ds(pl.multiple_of(r * L + start, BAND_HALF), win)
                if g == 0:
                    q = q_ref[0, jj * tq:(jj + 1) * tq, :]
                    kwin = k_ref[0, kv_rows, :]
                    rows = pl.ds(jj * tq, tq)
                else:
                    q = qc[g - 1, r * n + jj * tq:r * n + (jj + 1) * tq, :]
                    kwin = kc[g - 1, kv_rows, :]
                    rows = pl.ds(jj * tq * d + r, tq, stride=d)
                bias = bias_ref[(i0 - start) // BAND_HALF]
                zero = jnp.zeros_like(q)
                for hd, vref in enumerate((vac, vbc)):
                    qm = jnp.where(lo, q, zero) if hd == 0 else jnp.where(lo, zero, q)
                    s = lax.dot_general(qm, kwin, (((1,), (1,)), ((), ())),
                                        preferred_element_type=F32) + bias
                    m = jnp.max(s, axis=1, keepdims=True)
                    p = jnp.exp2(s - m).astype(BF)
                    acc[g, hd, rows, :] = jnp.dot(p, vref[g, kv_rows, :],
                                                  preferred_element_type=F32)
                    mx[g, hd, rows, :] = jnp.broadcast_to(m, (tq, LANES))

    chunk = 2 * tq
    lo_c = lax.broadcasted_iota(jnp.int32, (chunk, LANES), 1) < HEAD_DIM
    n_pat = len(DIL_STRIDES)
    for c in range(half // chunk):
        rows = slice(c * chunk, (c + 1) * chunk)
        outs = []
        for hd in range(2):
            m = [mx[g, hd, rows, :] for g in range(n_pat)]
            m_all = functools.reduce(jnp.maximum, m)
            tot = sum(jnp.exp2(m[g] - m_all) * acc[g, hd, rows, :] for g in range(n_pat))
            outs.append(tot * (1.0 / pltpu.roll(tot, HEAD_DIM, axis=1)))
        o_ref[0, rows, :] = jnp.where(lo_c, outs[0], outs[1]).astype(BF)


def _dilated_attention(qd, kd, vd):
    B, S, W = qd.shape
    assert all((w // 2) // d == BAND_HALF for w, d in DIL_PATTERNS)
    assert DIL_STRIDES == (1, 4, 16) and S // DIL_STRIDES[-1] == BAND_WIN
    assert (S // 2) % (TQ_BAND * DIL_STRIDES[-1]) == 0
    half = S // 2
    bias = _band_bias()
    n_pat = len(DIL_STRIDES)
    return pl.pallas_call(
        functools.partial(_dil_body, S=S),
        grid=(B, W // LANES, 2),
        in_specs=[
            pl.BlockSpec((1, half, LANES), lambda b, p, h: (b, h, p)),
            pl.BlockSpec((1, S, LANES), lambda b, p, h: (b, 0, p)),
            pl.BlockSpec((1, S, LANES), lambda b, p, h: (b, 0, p)),
            _resident(bias.shape),
        ],
        out_specs=pl.BlockSpec((1, half, LANES), lambda b, p, h: (b, h, p)),
        out_shape=jax.ShapeDtypeStruct((B, S, W), BF),
        scratch_shapes=[
            pltpu.VMEM((S, LANES), F32),
            pltpu.VMEM((S, LANES), F32),
            pltpu.VMEM((n_pat - 1, S, LANES), BF),
            pltpu.VMEM((n_pat, S, LANES), BF),
            pltpu.VMEM((n_pat, S, LANES), BF),
            pltpu.VMEM((n_pat - 1, half, LANES), BF),
            pltpu.VMEM((n_pat, 2, half, LANES), F32),
            pltpu.VMEM((n_pat, 2, half, LANES), F32),
        ],
        compiler_params=_cparams(3),
        name="dilated_attn",
    )(qd, kd, vd, bias)


NA_ROWS_WIN = 10
NA_DR = 2 * NA_KH - 1
NA_DC = 2 * NA_KW - 1


def _bias2_body(rpb_ref, o_ref):
    cq = lax.broadcasted_iota(jnp.int32, (GRID_W, LANES), 0)
    lane = lax.broadcasted_iota(jnp.int32, (GRID_W, LANES), 1)
    kc = lane % GRID_W
    hi = lane >= GRID_W
    c0 = jnp.clip(cq - NA_KW // 2, 0, GRID_W - NA_KW)
    col_ok = (kc >= c0) & (kc < c0 + NA_KW)
    for h in range(H_NA):
        for d in range(NA_DR + 1):
            rows = [jnp.broadcast_to(rpb_ref[h, d + k:d + k + 1, :], (GRID_W, LANES)) for k in (0, 1)]
            a = pltpu.roll(rows[0], LANES - (NA_KW - 1), axis=1, stride=1, stride_axis=0)
            b = pltpu.roll(rows[1], GRID_W - (NA_KW - 1), axis=1, stride=1, stride_axis=0)
            ok = col_ok
            if d == 0:
                ok = col_ok & hi
            elif d == NA_DR:
                ok = col_ok & jnp.logical_not(hi)
            o_ref[h, d] = jnp.where(ok, jnp.where(hi, b, a) * LOG2E, NEG_INF)


def _bias2(rpb_l):
    h, ndr, ndc = rpb_l.shape
    rpb_pad = jnp.pad(rpb_l, ((0, 0), (1, 1), (0, LANES - ndc)))
    shape = (H_NA, NA_DR + 1, GRID_W, LANES)
    return pl.pallas_call(
        _bias2_body,
        grid=(1,),
        in_specs=[_resident(rpb_pad.shape)],
        out_specs=pl.BlockSpec(shape, lambda i: (0, 0, 0, 0)),
        out_shape=jax.ShapeDtypeStruct(shape, F32),
        compiler_params=_cparams(1),
        name="nbr_bias",
    )(rpb_pad)


NA_TILE_VARIANTS = 5
NA_TILES_PER_STEP = 16


def _nbr_window_row(j, rows):
    return jnp.clip(2 * j - NA_KH // 2, 0, rows - NA_ROWS_WIN)


def _tile_bias_body(b2_ref, o_ref, *, rows):
    v = pl.program_id(0)
    n_tiles = rows // 2
    j = jnp.where(v < 2, v, jnp.where(v == 2, 2, v + (n_tiles - NA_TILE_VARIANTS)))
    rs = _nbr_window_row(j, rows)
    lane_b = lax.broadcasted_iota(jnp.int32, (GRID_W, LANES), 1) // GRID_W
    for h in range(H_NA):
        bias_rows = []
        for qi in range(2):
            rq = 2 * j + qi
            r0 = jnp.clip(rq - NA_KH // 2, 0, rows - NA_KH)
            blocks = []
            for kk in range(NA_ROWS_WIN // 2):
                kr = rs + 2 * kk
                d = jnp.clip(kr - rq + NA_KH, 0, NA_DR)
                krl = kr + lane_b
                ok = (krl >= r0) & (krl < r0 + NA_KH)
                blocks.append(jnp.where(ok, b2_ref[h, d], NEG_INF))
            bias_rows.append(jnp.concatenate(blocks, axis=1))
        o_ref[0, h] = jnp.concatenate(bias_rows, axis=0)


def _tile_bias(bias2, rows):
    assert rows // 2 > NA_TILE_VARIANTS
    win = NA_ROWS_WIN * GRID_W
    return pl.pallas_call(
        functools.partial(_tile_bias_body, rows=rows),
        grid=(NA_TILE_VARIANTS,),
        in_specs=[_resident(bias2.shape)],
        out_specs=pl.BlockSpec((1, H_NA, TQ_BAND, win), lambda v: (v, 0, 0, 0)),
        out_shape=jax.ShapeDtypeStruct((NA_TILE_VARIANTS, H_NA, TQ_BAND, win), F32),
        compiler_params=_cparams(1),
        name="nbr_tile_bias",
    )(bias2)


def _nbr_body(q_ref, k_ref, v_ref, bias_ref, o_ref, va, vb, *, rows):
    tq = TQ_BAND
    win = NA_ROWS_WIN * GRID_W
    n_tiles = rows // 2
    S = rows * GRID_W
    jt = pl.program_id(1)

    @pl.when(jt == 0)
    def _():
        lo_s = lax.broadcasted_iota(jnp.int32, (S, LANES), 1) < HEAD_DIM
        for hp in range(H_NA // 2):
            v = v_ref[0, :, hp * LANES:(hp + 1) * LANES]
            va[hp] = jnp.where(lo_s, v, jnp.ones_like(v))
            vb[hp] = jnp.where(lo_s, jnp.ones_like(v), v)

    lo = lax.broadcasted_iota(jnp.int32, (tq, LANES), 1) < HEAD_DIM
    jobs = []
    for t in range(NA_TILES_PER_STEP):
        j = jt * NA_TILES_PER_STEP + t
        start = pl.multiple_of(_nbr_window_row(j, rows) * GRID_W, 2 * GRID_W)
        var = jnp.where(j < 2, j, jnp.where(j >= n_tiles - 2, j - (n_tiles - NA_TILE_VARIANTS), 2))
        for hp in range(H_NA // 2):
            jobs.append((t, hp, start, var))
    scores = []
    for t, hp, start, var in jobs:
        cols = slice(hp * LANES, (hp + 1) * LANES)
        q = q_ref[0, t * tq:(t + 1) * tq, cols]
        zero = jnp.zeros_like(q)
        qs = jnp.concatenate([jnp.where(lo, q, zero), jnp.where(lo, zero, q)], axis=0)
        kw = k_ref[0, pl.ds(start, win), cols]
        bias = jnp.concatenate([bias_ref[var, 2 * hp], bias_ref[var, 2 * hp + 1]], axis=0)
        scores.append(lax.dot_general(qs, kw, (((1,), (1,)), ((), ())),
                                      preferred_element_type=F32) + bias)
    probs = [jnp.exp2(s - jnp.max(s, axis=1, keepdims=True)).astype(BF) for s in scores]
    for (t, hp, start, var), p in zip(jobs, probs):
        outs = []
        for hh, vref in enumerate((va, vb)):
            a = jnp.dot(p[hh * tq:(hh + 1) * tq], vref[hp, pl.ds(start, win), :],
                        preferred_element_type=F32)
            outs.append(a * (1.0 / pltpu.roll(a, HEAD_DIM, axis=1)))
        o_ref[0, t * tq:(t + 1) * tq, hp * LANES:(hp + 1) * LANES] = (
            jnp.where(lo, outs[0], outs[1]).astype(BF))


def _nbr_attention(qa, ka, va, bias2):
    B, S, W = qa.shape
    rows = S // GRID_W
    tq = TQ_BAND * NA_TILES_PER_STEP
    assert S % tq == 0
    bias_full = _tile_bias(bias2, rows)
    qblk = pl.BlockSpec((1, tq, W), lambda b, j: (b, j, 0))
    kblk = pl.BlockSpec((1, S, W), lambda b, j: (b, 0, 0))
    return pl.pallas_call(
        functools.partial(_nbr_body, rows=rows),
        grid=(B, S // tq),
        in_specs=[qblk, kblk, kblk, _resident(bias_full.shape)],
        out_specs=qblk,
        scratch_shapes=[pltpu.VMEM((H_NA // 2, S, LANES), BF) for _ in range(2)],
        out_shape=jax.ShapeDtypeStruct((B, S, W), BF),
        compiler_params=_cparams(2),
        name="nbr_attn",
    )(qa, ka, va, bias_full)


def _gqa_body(q_ref, kv_ref, o_ref, *, S):
    tq, tk = TQ_GQA, TK_GQA
    g = H_GQ // H_GKV
    lane = lax.broadcasted_iota(jnp.int32, (tq, LANES), 1)
    lo = lane < HEAD_DIM
    qs = []
    for kvh in range(H_GKV):
        mine = lo if kvh == 0 else jnp.logical_not(lo)
        parts = []
        for i in range(g):
            h = g * kvh + i
            chunk = q_ref[0, :, (h // 2) * LANES:(h // 2 + 1) * LANES]
            if (h % 2) != kvh:
                chunk = pltpu.roll(chunk.astype(F32), HEAD_DIM, axis=1).astype(BF)
            parts.append(jnp.where(mine, chunk, jnp.zeros_like(chunk)))
        qs.append(jnp.concatenate(parts, axis=0))

    def step(c, carry):
        rows = pl.ds(pl.multiple_of(c * tk, tk), tk)
        kblk = kv_ref[0, rows, 0:LANES]
        new = []
        for kvh in range(H_GKV):
            m, acc = carry[kvh]
            vblk = kv_ref[0, rows, (1 + kvh) * LANES:(2 + kvh) * LANES]
            s = lax.dot_general(qs[kvh], kblk, (((1,), (1,)), ((), ())),
                                preferred_element_type=F32)
            m_new = jnp.maximum(m, jnp.max(s, axis=1, keepdims=True))
            p = jnp.exp2(s - m_new)
            acc = jnp.exp2(m - m_new) * acc + jnp.dot(p.astype(BF), vblk,
                                                      preferred_element_type=F32)
            new.append((m_new, acc))
        return tuple(new)

    init = tuple((jnp.full((g * tq, 1), NEG_INF, F32), jnp.zeros((g * tq, LANES), F32))
                 for _ in range(H_GKV))
    fin = lax.fori_loop(0, S // tk, step, init, unroll=True)
    res = [acc * (1.0 / pltpu.roll(acc, HEAD_DIM, axis=1)) for _, acc in fin]
    for c in range(GQ_W // LANES):
        halves = []
        for hh in range(2):
            h = 2 * c + hh
            kvh, i = h // g, h % g
            v = res[kvh][i * tq:(i + 1) * tq]
            if kvh != hh:
                v = pltpu.roll(v, HEAD_DIM, axis=1)
            halves.append(v)
        o_ref[0, :, c * LANES:(c + 1) * LANES] = jnp.where(lo, halves[0], halves[1]).astype(BF)


def _gqa_attention(qc, kvc):
    B, S, W = qc.shape
    qblk = pl.BlockSpec((1, TQ_GQA, W), lambda b, j: (b, j, 0))
    kvblk = pl.BlockSpec((1, S, kvc.shape[-1]), lambda b, j: (b, 0, 0))
    return pl.pallas_call(
        functools.partial(_gqa_body, S=S),
        grid=(B, S // TQ_GQA),
        in_specs=[qblk, kvblk],
        out_specs=qblk,
        out_shape=jax.ShapeDtypeStruct((B, S, W), BF),
        compiler_params=_cparams(2),
        name="gqa_attn",
    )(qc, kvc)


def _ffn_body(*refs, tiles_per_seq, d_ff):
    (oa, oa_p, oa_n, od, od_p, od_n, oc, oc_p, oc_n, x, x_p, x_n,
     og_ref, wo_ref, g2_ref, wgu_ref, cw_ref, cb_ref, wd_ref, o_ref, hext, gs, act) = refs
    tm, halo = TM_FFN, FFN_HALO
    i = pl.program_id(0) % tiles_per_seq

    def ext(main, prev, nxt):
        return jnp.concatenate([prev[...], main[...], nxt[...]], axis=0)

    na = _rms(ext(oa, oa_p, oa_n).astype(F32), og_ref[:, :NA_W])
    nd = _rms(ext(od, od_p, od_n).astype(F32), og_ref[:, NA_W:NA_W + DIL_W])
    nc = _rms(ext(oc, oc_p, oc_n).astype(F32), og_ref[:, NA_W + DIL_W:])
    mix = jnp.concatenate([na, nd, nc], axis=1).astype(BF)
    y_ext = ext(x, x_p, x_n) + jnp.dot(mix, wo_ref[...], preferred_element_type=F32)
    y = y_ext[halo:halo + tm]
    row = lax.broadcasted_iota(jnp.int32, (tm + 2 * halo, 1), 0)
    keep = ((row >= halo) | (i > 0)) & ((row < halo + tm) | (i < tiles_per_seq - 1))
    hext[...] = jnp.where(keep, _rms(y_ext, g2_ref[...]), 0.0).astype(BF)
    for c in range(d_ff // MXU_N):
        cols = slice(c * MXU_N, (c + 1) * MXU_N)
        ucols = slice(d_ff + c * MXU_N, d_ff + (c + 1) * MXU_N)
        gs[...] = jnp.dot(hext[...], wgu_ref[:, cols], preferred_element_type=F32)
        u = jnp.dot(hext[halo:halo + tm], wgu_ref[:, ucols], preferred_element_type=F32)
        gc = gs[halo - 1:halo - 1 + tm] * cw_ref[0:1, cols] + cb_ref[:, cols]
        gc = gc + gs[halo:halo + tm] * cw_ref[1:2, cols]
        gc = gc + gs[halo + 1:halo + 1 + tm] * cw_ref[2:3, cols]
        gelu = 0.5 * gc * (1.0 + lax.erf(gc * (0.5 ** 0.5)))
        act[:, cols] = (gelu * u).astype(BF)
    o_ref[...] = y + jnp.dot(act[...], wd_ref[...], preferred_element_type=F32)


def _ffn(oa, od, oc, x2, og, w_out, g2, wgu, cw, cb, wd, l, S):
    N, D = x2.shape
    d_ff = wd.shape[1]
    tm, halo = TM_FFN, FFN_HALO
    assert S % tm == 0 and d_ff % MXU_N == 0
    r = tm // halo
    nblk = N // halo
    row = lambda i: (i, 0)
    prev = lambda i: (jnp.maximum(i * r - 1, 0), 0)
    nxt = lambda i: (jnp.minimum((i + 1) * r, nblk - 1), 0)
    tiled, args = [], []
    for a in (oa, od, oc, x2):
        w = a.shape[1]
        tiled += [pl.BlockSpec((tm, w), row), pl.BlockSpec((halo, w), prev),
                  pl.BlockSpec((halo, w), nxt)]
        args += [a, a, a]
    params = (og, w_out, g2, wgu, cw, cb, wd)
    return pl.pallas_call(
        functools.partial(_ffn_body, tiles_per_seq=S // tm, d_ff=d_ff),
        grid=(N // tm,),
        in_specs=tiled + [_resident_layer(p.shape, l) if p.ndim == 3 else _resident(p.shape)
                          for p in params],
        out_specs=pl.BlockSpec((tm, D), row),
        out_shape=jax.ShapeDtypeStruct((N, D), F32),
        scratch_shapes=[
            pltpu.VMEM((tm + 2 * halo, D), BF),
            pltpu.VMEM((tm + 2 * halo, MXU_N), F32),
            pltpu.VMEM((tm, d_ff), BF),
        ],
        compiler_params=_cparams(1),
        name="ffn",
    )(*args, *params)


def _rope_cos_sin(pos, dim, theta):
    inv = theta ** (-jnp.arange(0, dim, 2, dtype=F32) / dim)
    ang = inv[:, None] * pos.astype(F32)[None, :]
    cos, sin = lax.optimization_barrier((jnp.cos(ang), jnp.sin(ang)))
    return cos.T, sin.T


def _rotary_tables(S):
    t = jnp.arange(S, dtype=jnp.int32)
    cos1, sin1 = _rope_cos_sin(t, ROPE_DIMS, ROPE_THETA)
    rest = HEAD_DIM - ROPE_DIMS
    cb = jnp.concatenate([cos1, cos1, jnp.ones((S, rest), F32)], axis=1)
    sb = jnp.concatenate([-sin1, sin1, jnp.zeros((S, rest), F32)], axis=1)
    cg, sg = _rope_cos_sin(jnp.arange(GRID_W, dtype=jnp.int32), HEAD_DIM // 2, AXIAL_THETA)
    cr, sr = (jnp.repeat(a, S // GRID_W, axis=0) for a in (cg, sg))
    cc, sc = (jnp.tile(a, (S // GRID_W, 1)) for a in (cg, sg))
    ca = jnp.concatenate([cr, cr, cc, cc], axis=1)
    sa = jnp.concatenate([-sr, sr, -sc, sc], axis=1)
    two = lambda a: jnp.concatenate([a, a], axis=1)
    return two(cb), two(sb), two(ca), two(sa)


def _permute_in_columns(w):
    sizes = (NA_W, NA_W, NA_W, DIL_W, DIL_W, DIL_W, GQ_W, GKV_W, GKV_W)
    offs = [0]
    for s in sizes:
        offs.append(offs[-1] + s)
    seg = [w[..., offs[i]:offs[i + 1]] for i in range(len(sizes))]
    qa, ka, va, qd, kd, vd, qc, kc, vc = seg
    return jnp.concatenate([qa, ka, qd, kd, qc, kc, va, vd, vc], axis=-1)


def _gain_vector(qg, kg):
    scale = HEAD_DIM ** -0.5
    parts = [
        jnp.tile(qg[0] * (scale * LOG2E), H_NA), jnp.tile(kg[0], H_NA),
        jnp.tile(qg[1] * (scale * LOG2E), H_DIL), jnp.tile(kg[1], H_DIL),
        jnp.tile(qg[2] * (scale * LOG2E), H_GQ), jnp.tile(kg[2], H_GKV),
    ]
    return jnp.concatenate(parts)[None, :]


def kernel(x, norm1_g, w_in, q_norm_g, k_norm_g, rpb, out_norm_g, w_out, norm2_g,
           w_gate_up, conv_w, conv_b, w_down):
    B, S, D = x.shape
    depth = w_in.shape[0]
    N = B * S
    assert S == GRID_W * GRID_W and S % TQ_GQA == 0 and S % TK_GQA == 0
    tabs = _rotary_tables(S)
    idx = jnp.arange(MXU_N) // HEAD_DIM
    gsum = (idx[:, None] == idx[None, :]).astype(BF)
    x2 = x.reshape(N, D)
    w_perm = _permute_in_columns(w_in.astype(BF))
    w_out, w_gate_up, w_down = (w.astype(BF) for w in (w_out, w_gate_up, w_down))
    for l in range(depth):
        gain = _gain_vector(q_norm_g[l], k_norm_g[l])
        qa, ka, va, qd, kd, vd, qc, kvc = _inproj(
            x2, norm1_g[l][None, :], w_perm, l, gsum, gain, tabs, S)
        seq = lambda a: a.reshape(B, S, a.shape[-1])
        out_a = _nbr_attention(seq(qa), seq(ka), seq(va), _bias2(rpb[l]))
        out_d = _dilated_attention(seq(qd), seq(kd), seq(vd))
        out_c = _gqa_attention(seq(qc), seq(kvc))
        x2 = _ffn(out_a.reshape(N, NA_W), out_d.reshape(N, DIL_W), out_c.reshape(N, GQ_W), x2,
                  out_norm_g[l][None, :], w_out, norm2_g[l][None, :],
                  w_gate_up, conv_w[l], conv_b[l][None, :], w_down, l, S)
    return x2.reshape(B, S, D)
```

```python
import functools

import jax
import jax.numpy as jnp
from jax import lax
from jax.experimental import pallas as pl
from jax.experimental.pallas import tpu as pltpu

BF = jnp.bfloat16
F32 = jnp.float32

HEAD_DIM = 64
H_NA, H_DIL, H_GQ, H_GKV = 4, 6, 6, 2
NA_W, DIL_W, GQ_W, GKV_W = 256, 384, 384, 128
GRID_W = 64
NA_KH = 8
NA_KW = 16
DIL_PATTERNS = ((128, 1), (512, 4), (2048, 16))
BAND_HALF = 64
ROPE_THETA = 500000.0
ROPE_DIMS = 16
AXIAL_THETA = 10000.0
EPS = 1e-6
NEG_INF = -1e30
LOG2E = 1.4426950408889634

LANES = 128
MXU_N = 256
VMEM_LIMIT = 56 * 1024 * 1024

TM_IN = 1024
PROJ_SUB = 256
TM_FFN = 1024
FFN_HALO = 16
TQ_BAND = 128
BAND_WIN = 256
TQ_GQA = 256
TK_GQA = 2048


def _cparams(n_axes):
    return pltpu.CompilerParams(
        dimension_semantics=("arbitrary",) * n_axes, vmem_limit_bytes=VMEM_LIMIT)


def _rms(v, g):
    ms = jnp.mean(v * v, axis=-1, keepdims=True)
    return v * lax.rsqrt(ms + EPS) * g


def _resident(shape):
    nd = len(shape)
    return pl.BlockSpec(shape, lambda *_: (0,) * nd, pipeline_mode=pl.Buffered(1))


def _resident_layer(stacked_shape, l):
    nd = len(stacked_shape)
    return pl.BlockSpec((None,) + tuple(stacked_shape[1:]), lambda *_: (l,) + (0,) * (nd - 1),
                        pipeline_mode=pl.Buffered(1))


def _rotary(p, cos, sin, first, half):
    up = pltpu.roll(p, LANES - half, axis=1)
    dn = pltpu.roll(p, half, axis=1)
    return p * cos + jnp.where(first, up, dn) * sin


def _inproj_body(x_ref, g1_ref, w_ref, gs_ref, gain_ref, cb_ref, sb_ref, cc_ref, sc_ref,
                 qa_ref, ka_ref, va_ref, qd_ref, kd_ref, vd_ref, qc_ref, kvc_ref):
    lane = lax.broadcasted_iota(jnp.int32, (1, LANES), 1)
    first_b = (lane % HEAD_DIM) < (ROPE_DIMS // 2)
    first_c = (lane % (HEAD_DIM // 2)) < (HEAD_DIM // 4)
    lo_half = lane < HEAD_DIM
    segments = ((qa_ref, NA_W, True, None), (ka_ref, NA_W, True, None), (va_ref, NA_W, False, None),
                (qd_ref, DIL_W, True, "b"), (kd_ref, DIL_W, True, "b"), (vd_ref, DIL_W, False, None),
                (qc_ref, GQ_W, True, "c"), (kvc_ref, GKV_W, True, "c"), (None, GKV_W, False, None))
    pieces, col = [], 0
    for ref, width, normed, rot in segments:
        for off in range(0, width, LANES):
            pieces.append((col + off, ref, off, normed, rot))
        col += width
    norm_chunks = [c for c, _, _, normed, _ in pieces if normed][::MXU_N // LANES]
    assert all(pieces[c // LANES + 1][3] for c in norm_chunks)
    for r0 in range(0, x_ref.shape[0], PROJ_SUB):
        rows = slice(r0, r0 + PROJ_SUB)
        h = _rms(x_ref[rows, :], g1_ref[...]).astype(BF)
        p = jnp.dot(h, w_ref[...], preferred_element_type=F32)
        sqs = [(p[:, c:c + MXU_N] * p[:, c:c + MXU_N]).astype(BF) for c in norm_chunks]
        sss = [jnp.dot(sq, gs_ref[...], preferred_element_type=F32) for sq in sqs]
        scaled = {}
        for c, ss in zip(norm_chunks, sss):
            blk = p[:, c:c + MXU_N] * lax.rsqrt(ss * (1.0 / HEAD_DIM) + EPS) * gain_ref[:, c:c + MXU_N]
            scaled[c] = blk[:, :LANES]
            scaled[c + LANES] = blk[:, LANES:]
        for c, ref, off, normed, rot in pieces:
            v = scaled[c] if normed else p[:, c:c + LANES]
            if rot == "b":
                v = _rotary(v, cb_ref[rows, :], sb_ref[rows, :], first_b, ROPE_DIMS // 2)
            elif rot == "c":
                v = _rotary(v, cc_ref[rows, :], sc_ref[rows, :], first_c, HEAD_DIM // 4)
            if ref is None:
                kvc_ref[rows, LANES:2 * LANES] = jnp.where(lo_half, v, 1.0).astype(BF)
                kvc_ref[rows, 2 * LANES:3 * LANES] = jnp.where(lo_half, 1.0, v).astype(BF)
            else:
                ref[rows, off:off + LANES] = v.astype(BF)


def _inproj(x2, g1, w_stack, l, gsum, gain, tabs, S):
    N, D = x2.shape
    tm = TM_IN
    assert S % tm == 0 and tm % PROJ_SUB == 0
    nt_seq = S // tm
    row = lambda i: (i, 0)
    tab = pl.BlockSpec((tm, LANES), lambda i: (i % nt_seq, 0))
    widths = (NA_W, NA_W, NA_W, DIL_W, DIL_W, DIL_W, GQ_W, 3 * GKV_W)
    return pl.pallas_call(
        _inproj_body,
        grid=(N // tm,),
        in_specs=[
            pl.BlockSpec((tm, D), row),
            _resident(g1.shape),
            _resident_layer(w_stack.shape, l),
            _resident(gsum.shape),
            _resident(gain.shape),
            tab, tab, tab, tab,
        ],
        out_specs=[pl.BlockSpec((tm, w), row) for w in widths],
        out_shape=[jax.ShapeDtypeStruct((N, w), BF) for w in widths],
        compiler_params=_cparams(1),
        name="inproj",
    )(x2, g1, w_stack, gsum, gain, *tabs)


DIL_STRIDES = tuple(d for _, d in DIL_PATTERNS)


def _band_bias():
    r = jnp.arange(TQ_BAND)[:, None]
    c = jnp.arange(BAND_WIN)[None, :]
    masks = [jnp.where(jnp.abs(r + off - c) <= BAND_HALF, 0.0, NEG_INF).astype(F32)
             for off in (0, BAND_HALF, 2 * BAND_HALF)]
    return jnp.stack(masks)


def _dil_body(q_ref, k_ref, v_ref, bias_ref, o_ref,
              stage, stage4, kc, vac, vbc, qc, acc, mx, *, S):
    tq, win = TQ_BAND, BAND_WIN
    half = S // 2
    hf = pl.program_id(2)
    lo = lax.broadcasted_iota(jnp.int32, (tq, LANES), 1) < HEAD_DIM

    def deinterleave(dsts, g0, rows):
        n4, n16 = rows // 4, rows // 16
        for r in range(4):
            cls = stage[pl.ds(r, n4, stride=4), :]
            stage4[r * n4:(r + 1) * n4, :] = cls
            for ref, fn in dsts:
                ref[g0, r * n4:(r + 1) * n4, :] = fn(cls).astype(BF)
        for r in range(16):
            cls = stage4[pl.ds((r % 4) * n4 + r // 4, n16, stride=4), :]
            for ref, fn in dsts:
                ref[g0 + 1, r * n16:(r + 1) * n16, :] = fn(cls).astype(BF)

    def lo_of(a):
        return lax.broadcasted_iota(jnp.int32, a.shape, 1) < HEAD_DIM

    same = lambda a: a
    ones_right = lambda a: jnp.where(lo_of(a), a, 1.0)
    ones_left = lambda a: jnp.where(lo_of(a), 1.0, a)

    @pl.when(hf == 0)
    def _():
        stage[...] = k_ref[0].astype(F32)
        deinterleave([(kc, same)], 0, S)
        v = v_ref[0]
        stage[...] = v.astype(F32)
        vac[0] = ones_right(v).astype(BF)
        vbc[0] = ones_left(v).astype(BF)
        deinterleave([(vac, ones_right), (vbc, ones_left)], 1, S)

    stage[0:half] = q_ref[0].astype(F32)
    deinterleave([(qc, same)], 0, half)

    for g, d in enumerate(DIL_STRIDES):
        L = S // d
        n = half // d
        for r in range(d):
            for jj in range(n // tq):
                i0 = hf * n + jj * tq
                start = jnp.clip(i0 - BAND_HALF, 0, L - win)
                kv_rows = pl.ds(pl.multiple_of(r * L + start, BAND_HALF), win)
                if g == 0:
                    q = q_ref[0, jj * tq:(jj + 1) * tq, :]
                    kwin = k_ref[0, kv_rows, :]
                    rows = pl.ds(jj * tq, tq)
                else:
                    q = qc[g - 1, r * n + jj * tq:r * n + (jj + 1) * tq, :]
                    kwin = kc[g - 1, kv_rows, :]
                    rows = pl.ds(jj * tq * d + r, tq, stride=d)
                bias = bias_ref[(i0 - start) // BAND_HALF]
                zero = jnp.zeros_like(q)
                for hd, vref in enumerate((vac, vbc)):
                    qm = jnp.where(lo, q, zero) if hd == 0 else jnp.where(lo, zero, q)
                    s = lax.dot_general(qm, kwin, (((1,), (1,)), ((), ())),
                                        preferred_element_type=F32) + bias
                    m = jnp.max(s, axis=1, keepdims=True)
                    p = jnp.exp2(s - m).astype(BF)
                    acc[g, hd, rows, :] = jnp.dot(p, vref[g, kv_rows, :],
                                                  preferred_element_type=F32)
                    mx[g, hd, rows, :] = jnp.broadcast_to(m, (tq, LANES))

    chunk = 2 * tq
    lo_c = lax.broadcasted_iota(jnp.int32, (chunk, LANES), 1) < HEAD_DIM
    n_pat = len(DIL_STRIDES)
    for c in range(half // chunk):
        rows = slice(c * chunk, (c + 1) * chunk)
        outs = []
        for hd in range(2):
            m = [mx[g, hd, rows, :] for g in range(n_pat)]
            m_all = functools.reduce(jnp.maximum, m)
            tot = sum(jnp.exp2(m[g] - m_all) * acc[g, hd, rows, :] for g in range(n_pat))
            outs.append(tot * (1.0 / pltpu.roll(tot, HEAD_DIM, axis=1)))
        o_ref[0, rows, :] = jnp.where(lo_c, outs[0], outs[1]).astype(BF)


def _dilated_attention(qd, kd, vd):
    B, S, W = qd.shape
    assert all((w // 2) // d == BAND_HALF for w, d in DIL_PATTERNS)
    assert DIL_STRIDES == (1, 4, 16) and S // DIL_STRIDES[-1] == BAND_WIN
    assert (S // 2) % (TQ_BAND * DIL_STRIDES[-1]) == 0
    half = S // 2
    bias = _band_bias()
    n_pat = len(DIL_STRIDES)
    return pl.pallas_call(
        functools.partial(_dil_body, S=S),
        grid=(B, W // LANES, 2),
        in_specs=[
            pl.BlockSpec((1, half, LANES), lambda b, p, h: (b, h, p)),
            pl.BlockSpec((1, S, LANES), lambda b, p, h: (b, 0, p)),
            pl.BlockSpec((1, S, LANES), lambda b, p, h: (b, 0, p)),
            _resident(bias.shape),
        ],
        out_specs=pl.BlockSpec((1, half, LANES), lambda b, p, h: (b, h, p)),
        out_shape=jax.ShapeDtypeStruct((B, S, W), BF),
        scratch_shapes=[
            pltpu.VMEM((S, LANES), F32),
            pltpu.VMEM((S, LANES), F32),
            pltpu.VMEM((n_pat - 1, S, LANES), BF),
            pltpu.VMEM((n_pat, S, LANES), BF),
            pltpu.VMEM((n_pat, S, LANES), BF),
            pltpu.VMEM((n_pat - 1, half, LANES), BF),
            pltpu.VMEM((n_pat, 2, half, LANES), F32),
            pltpu.VMEM((n_pat, 2, half, LANES), F32),
        ],
        compiler_params=_cparams(3),
        name="dilated_attn",
    )(qd, kd, vd, bias)


NA_ROWS_WIN = 10
NA_DR = 2 * NA_KH - 1
NA_DC = 2 * NA_KW - 1


def _bias2_body(rpb_ref, o_ref):
    cq = lax.broadcasted_iota(jnp.int32, (GRID_W, LANES), 0)
    lane = lax.broadcasted_iota(jnp.int32, (GRID_W, LANES), 1)
    kc = lane % GRID_W
    hi = lane >= GRID_W
    c0 = jnp.clip(cq - NA_KW // 2, 0, GRID_W - NA_KW)
    col_ok = (kc >= c0) & (kc < c0 + NA_KW)
    for h in range(H_NA):
        for d in range(NA_DR + 1):
            rows = [jnp.broadcast_to(rpb_ref[h, d + k:d + k + 1, :], (GRID_W, LANES)) for k in (0, 1)]
            a = pltpu.roll(rows[0], LANES - (NA_KW - 1), axis=1, stride=1, stride_axis=0)
            b = pltpu.roll(rows[1], GRID_W - (NA_KW - 1), axis=1, stride=1, stride_axis=0)
            ok = col_ok
            if d == 0:
                ok = col_ok & hi
            elif d == NA_DR:
                ok = col_ok & jnp.logical_not(hi)
            o_ref[h, d] = jnp.where(ok, jnp.where(hi, b, a) * LOG2E, NEG_INF)


def _bias2(rpb_l):
    h, ndr, ndc = rpb_l.shape
    rpb_pad = jnp.pad(rpb_l, ((0, 0), (1, 1), (0, LANES - ndc)))
    shape = (H_NA, NA_DR + 1, GRID_W, LANES)
    return pl.pallas_call(
        _bias2_body,
        grid=(1,),
        in_specs=[_resident(rpb_pad.shape)],
        out_specs=pl.BlockSpec(shape, lambda i: (0, 0, 0, 0)),
        out_shape=jax.ShapeDtypeStruct(shape, F32),
        compiler_params=_cparams(1),
        name="nbr_bias",
    )(rpb_pad)


NA_TILE_VARIANTS = 5
NA_TILES_PER_STEP = 16


def _nbr_window_row(j, rows):
    return jnp.clip(2 * j - NA_KH // 2, 0, rows - NA_ROWS_WIN)


def _tile_bias_body(b2_ref, o_ref, *, rows):
    v = pl.program_id(0)
    n_tiles = rows // 2
    j = jnp.where(v < 2, v, jnp.where(v == 2, 2, v + (n_tiles - NA_TILE_VARIANTS)))
    rs = _nbr_window_row(j, rows)
    lane_b = lax.broadcasted_iota(jnp.int32, (GRID_W, LANES), 1) // GRID_W
    for h in range(H_NA):
        bias_rows = []
        for qi in range(2):
            rq = 2 * j + qi
            r0 = jnp.clip(rq - NA_KH // 2, 0, rows - NA_KH)
            blocks = []
            for kk in range(NA_ROWS_WIN // 2):
                kr = rs + 2 * kk
                d = jnp.clip(kr - rq + NA_KH, 0, NA_DR)
                krl = kr + lane_b
                ok = (krl >= r0) & (krl < r0 + NA_KH)
                blocks.append(jnp.where(ok, b2_ref[h, d], NEG_INF))
            bias_rows.append(jnp.concatenate(blocks, axis=1))
        o_ref[0, h] = jnp.concatenate(bias_rows, axis=0)


def _tile_bias(bias2, rows):
    assert rows // 2 > NA_TILE_VARIANTS
    win = NA_ROWS_WIN * GRID_W
    return pl.pallas_call(
        functools.partial(_tile_bias_body, rows=rows),
        grid=(NA_TILE_VARIANTS,),
        in_specs=[_resident(bias2.shape)],
        out_specs=pl.BlockSpec((1, H_NA, TQ_BAND, win), lambda v: (v, 0, 0, 0)),
        out_shape=jax.ShapeDtypeStruct((NA_TILE_VARIANTS, H_NA, TQ_BAND, win), F32),
        compiler_params=_cparams(1),
        name="nbr_tile_bias",
    )(bias2)


def _nbr_body(q_ref, k_ref, v_ref, bias_ref, o_ref, va, vb, *, rows):
    tq = TQ_BAND
    win = NA_ROWS_WIN * GRID_W
    n_tiles = rows // 2
    S = rows * GRID_W
    jt = pl.program_id(1)

    @pl.when(jt == 0)
    def _():
        lo_s = lax.broadcasted_iota(jnp.int32, (S, LANES), 1) < HEAD_DIM
        for hp in range(H_NA // 2):
            v = v_ref[0, :, hp * LANES:(hp + 1) * LANES]
            va[hp] = jnp.where(lo_s, v, jnp.ones_like(v))
            vb[hp] = jnp.where(lo_s, jnp.ones_like(v), v)

    lo = lax.broadcasted_iota(jnp.int32, (tq, LANES), 1) < HEAD_DIM
    jobs = []
    for t in range(NA_TILES_PER_STEP):
        j = jt * NA_TILES_PER_STEP + t
        start = pl.multiple_of(_nbr_window_row(j, rows) * GRID_W, 2 * GRID_W)
        var = jnp.where(j < 2, j, jnp.where(j >= n_tiles - 2, j - (n_tiles - NA_TILE_VARIANTS), 2))
        for hp in range(H_NA // 2):
            jobs.append((t, hp, start, var))
    scores = []
    for t, hp, start, var in jobs:
        cols = slice(hp * LANES, (hp + 1) * LANES)
        q = q_ref[0, t * tq:(t + 1) * tq, cols]
        zero = jnp.zeros_like(q)
        qs = jnp.concatenate([jnp.where(lo, q, zero), jnp.where(lo, zero, q)], axis=0)
        kw = k_ref[0, pl.ds(start, win), cols]
        bias = jnp.concatenate([bias_ref[var, 2 * hp], bias_ref[var, 2 * hp + 1]], axis=0)
        scores.append(lax.dot_general(qs, kw, (((1,), (1,)), ((), ())),
                                      preferred_element_type=F32) + bias)
    probs = [jnp.exp2(s - jnp.max(s, axis=1, keepdims=True)).astype(BF) for s in scores]
    for (t, hp, start, var), p in zip(jobs, probs):
        outs = []
        for hh, vref in enumerate((va, vb)):
            a = jnp.dot(p[hh * tq:(hh + 1) * tq], vref[hp, pl.ds(start, win), :],
                        preferred_element_type=F32)
            outs.append(a * (1.0 / pltpu.roll(a, HEAD_DIM, axis=1)))
        o_ref[0, t * tq:(t + 1) * tq, hp * LANES:(hp + 1) * LANES] = (
            jnp.where(lo, outs[0], outs[1]).astype(BF))


def _nbr_attention(qa, ka, va, bias2):
    B, S, W = qa.shape
    rows = S // GRID_W
    tq = TQ_BAND * NA_TILES_PER_STEP
    assert S % tq == 0
    bias_full = _tile_bias(bias2, rows)
    qblk = pl.BlockSpec((1, tq, W), lambda b, j: (b, j, 0))
    kblk = pl.BlockSpec((1, S, W), lambda b, j: (b, 0, 0))
    return pl.pallas_call(
        functools.partial(_nbr_body, rows=rows),
        grid=(B, S // tq),
        in_specs=[qblk, kblk, kblk, _resident(bias_full.shape)],
        out_specs=qblk,
        scratch_shapes=[pltpu.VMEM((H_NA // 2, S, LANES), BF) for _ in range(2)],
        out_shape=jax.ShapeDtypeStruct((B, S, W), BF),
        compiler_params=_cparams(2),
        name="nbr_attn",
    )(qa, ka, va, bias_full)


def _gqa_body(q_ref, kv_ref, o_ref, *, S):
    tq, tk = TQ_GQA, TK_GQA
    g = H_GQ // H_GKV
    lane = lax.broadcasted_iota(jnp.int32, (tq, LANES), 1)
    lo = lane < HEAD_DIM
    qs = []
    for kvh in range(H_GKV):
        mine = lo if kvh == 0 else jnp.logical_not(lo)
        parts = []
        for i in range(g):
            h = g * kvh + i
            chunk = q_ref[0, :, (h // 2) * LANES:(h // 2 + 1) * LANES]
            if (h % 2) != kvh:
                chunk = pltpu.roll(chunk.astype(F32), HEAD_DIM, axis=1).astype(BF)
            parts.append(jnp.where(mine, chunk, jnp.zeros_like(chunk)))
        qs.append(jnp.concatenate(parts, axis=0))

    def step(c, carry):
        rows = pl.ds(pl.multiple_of(c * tk, tk), tk)
        kblk = kv_ref[0, rows, 0:LANES]
        new = []
        for kvh in range(H_GKV):
            m, acc = carry[kvh]
            vblk = kv_ref[0, rows, (1 + kvh) * LANES:(2 + kvh) * LANES]
            s = lax.dot_general(qs[kvh], kblk, (((1,), (1,)), ((), ())),
                                preferred_element_type=F32)
            m_new = jnp.maximum(m, jnp.max(s, axis=1, keepdims=True))
            p = jnp.exp2(s - m_new)
            acc = jnp.exp2(m - m_new) * acc + jnp.dot(p.astype(BF), vblk,
                                                      preferred_element_type=F32)
            new.append((m_new, acc))
        return tuple(new)

    init = tuple((jnp.full((g * tq, 1), NEG_INF, F32), jnp.zeros((g * tq, LANES), F32))
                 for _ in range(H_GKV))
    fin = lax.fori_loop(0, S // tk, step, init, unroll=True)
    res = [acc * (1.0 / pltpu.roll(acc, HEAD_DIM, axis=1)) for _, acc in fin]
    for c in range(GQ_W // LANES):
        halves = []
        for hh in range(2):
            h = 2 * c + hh
            kvh, i = h // g, h % g
            v = res[kvh][i * tq:(i + 1) * tq]
            if kvh != hh:
                v = pltpu.roll(v, HEAD_DIM, axis=1)
            halves.append(v)
        o_ref[0, :, c * LANES:(c + 1) * LANES] = jnp.where(lo, halves[0], halves[1]).astype(BF)


def _gqa_attention(qc, kvc):
    B, S, W = qc.shape
    qblk = pl.BlockSpec((1, TQ_GQA, W), lambda b, j: (b, j, 0))
    kvblk = pl.BlockSpec((1, S, kvc.shape[-1]), lambda b, j: (b, 0, 0))
    return pl.pallas_call(
        functools.partial(_gqa_body, S=S),
        grid=(B, S // TQ_GQA),
        in_specs=[qblk, kvblk],
        out_specs=qblk,
        out_shape=jax.ShapeDtypeStruct((B, S, W), BF),
        compiler_params=_cparams(2),
        name="gqa_attn",
    )(qc, kvc)


def _ffn_body(*refs, tiles_per_seq, d_ff):
    (oa, oa_p, oa_n, od, od_p, od_n, oc, oc_p, oc_n, x, x_p, x_n,
     og_ref, wo_ref, g2_ref, wgu_ref, cw_ref, cb_ref, wd_ref, o_ref, hext, gs, act) = refs
    tm, halo = TM_FFN, FFN_HALO
    i = pl.program_id(0) % tiles_per_seq

    def ext(main, prev, nxt):
        return jnp.concatenate([prev[...], main[...], nxt[...]], axis=0)

    na = _rms(ext(oa, oa_p, oa_n).astype(F32), og_ref[:, :NA_W])
    nd = _rms(ext(od, od_p, od_n).astype(F32), og_ref[:, NA_W:NA_W + DIL_W])
    nc = _rms(ext(oc, oc_p, oc_n).astype(F32), og_ref[:, NA_W + DIL_W:])
    mix = jnp.concatenate([na, nd, nc], axis=1).astype(BF)
    y_ext = ext(x, x_p, x_n) + jnp.dot(mix, wo_ref[...], preferred_element_type=F32)
    y = y_ext[halo:halo + tm]
    row = lax.broadcasted_iota(jnp.int32, (tm + 2 * halo, 1), 0)
    keep = ((row >= halo) | (i > 0)) & ((row < halo + tm) | (i < tiles_per_seq - 1))
    hext[...] = jnp.where(keep, _rms(y_ext, g2_ref[...]), 0.0).astype(BF)
    for c in range(d_ff // MXU_N):
        cols = slice(c * MXU_N, (c + 1) * MXU_N)
        ucols = slice(d_ff + c * MXU_N, d_ff + (c + 1) * MXU_N)
        gs[...] = jnp.dot(hext[...], wgu_ref[:, cols], preferred_element_type=F32)
        u = jnp.dot(hext[halo:halo + tm], wgu_ref[:, ucols], preferred_element_type=F32)
        gc = gs[halo - 1:halo - 1 + tm] * cw_ref[0:1, cols] + cb_ref[:, cols]
        gc = gc + gs[halo:halo + tm] * cw_ref[1:2, cols]
        gc = gc + gs[halo + 1:halo + 1 + tm] * cw_ref[2:3, cols]
        gelu = 0.5 * gc * (1.0 + lax.erf(gc * (0.5 ** 0.5)))
        act[:, cols] = (gelu * u).astype(BF)
    o_ref[...] = y + jnp.dot(act[...], wd_ref[...], preferred_element_type=F32)


def _ffn(oa, od, oc, x2, og, w_out, g2, wgu, cw, cb, wd, l, S):
    N, D = x2.shape
    d_ff = wd.shape[1]
    tm, halo = TM_FFN, FFN_HALO
    assert S % tm == 0 and d_ff % MXU_N == 0
    r = tm // halo
    nblk = N // halo
    row = lambda i: (i, 0)
    prev = lambda i: (jnp.maximum(i * r - 1, 0), 0)
    nxt = lambda i: (jnp.minimum((i + 1) * r, nblk - 1), 0)
    tiled, args = [], []
    for a in (oa, od, oc, x2):
        w = a.shape[1]
        tiled += [pl.BlockSpec((tm, w), row), pl.BlockSpec((halo, w), prev),
                  pl.BlockSpec((halo, w), nxt)]
        args += [a, a, a]
    params = (og, w_out, g2, wgu, cw, cb, wd)
    return pl.pallas_call(
        functools.partial(_ffn_body, tiles_per_seq=S // tm, d_ff=d_ff),
        grid=(N // tm,),
        in_specs=tiled + [_resident_layer(p.shape, l) if p.ndim == 3 else _resident(p.shape)
                          for p in params],
        out_specs=pl.BlockSpec((tm, D), row),
        out_shape=jax.ShapeDtypeStruct((N, D), F32),
        scratch_shapes=[
            pltpu.VMEM((tm + 2 * halo, D), BF),
            pltpu.VMEM((tm + 2 * halo, MXU_N), F32),
            pltpu.VMEM((tm, d_ff), BF),
        ],
        compiler_params=_cparams(1),
        name="ffn",
    )(*args, *params)


def _rope_cos_sin(pos, dim, theta):
    inv = theta ** (-jnp.arange(0, dim, 2, dtype=F32) / dim)
    ang = inv[:, None] * pos.astype(F32)[None, :]
    cos, sin = lax.optimization_barrier((jnp.cos(ang), jnp.sin(ang)))
    return cos.T, sin.T


def _rotary_tables(S):
    t = jnp.arange(S, dtype=jnp.int32)
    cos1, sin1 = _rope_cos_sin(t, ROPE_DIMS, ROPE_THETA)
    rest = HEAD_DIM - ROPE_DIMS
    cb = jnp.concatenate([cos1, cos1, jnp.ones((S, rest), F32)], axis=1)
    sb = jnp.concatenate([-sin1, sin1, jnp.zeros((S, rest), F32)], axis=1)
    cg, sg = _rope_cos_sin(jnp.arange(GRID_W, dtype=jnp.int32), HEAD_DIM // 2, AXIAL_THETA)
    cr, sr = (jnp.repeat(a, S // GRID_W, axis=0) for a in (cg, sg))
    cc, sc = (jnp.tile(a, (S // GRID_W, 1)) for a in (cg, sg))
    ca = jnp.concatenate([cr, cr, cc, cc], axis=1)
    sa = jnp.concatenate([-sr, sr, -sc, sc], axis=1)
    two = lambda a: jnp.concatenate([a, a], axis=1)
    return two(cb), two(sb), two(ca), two(sa)


def _gain_vector(qg, kg):
    scale = HEAD_DIM ** -0.5
    parts = [
        jnp.tile(qg[0] * (scale * LOG2E), H_NA), jnp.tile(kg[0], H_NA), jnp.ones((NA_W,), F32),
        jnp.tile(qg[1] * (scale * LOG2E), H_DIL), jnp.tile(kg[1], H_DIL), jnp.ones((DIL_W,), F32),
        jnp.tile(qg[2] * (scale * LOG2E), H_GQ), jnp.tile(kg[2], H_GKV), jnp.ones((GKV_W,), F32),
    ]
    return jnp.concatenate(parts)[None, :]


def kernel(x, norm1_g, w_in, q_norm_g, k_norm_g, rpb, out_norm_g, w_out, norm2_g,
           w_gate_up, conv_w, conv_b, w_down):
    B, S, D = x.shape
    depth = w_in.shape[0]
    N = B * S
    assert S == GRID_W * GRID_W and S % TQ_GQA == 0 and S % TK_GQA == 0
    tabs = _rotary_tables(S)
    idx = jnp.arange(MXU_N) // HEAD_DIM
    gsum = (idx[:, None] == idx[None, :]).astype(BF)
    x2 = x.reshape(N, D)
    w_in, w_out, w_gate_up, w_down = (w.astype(BF) for w in (w_in, w_out, w_gate_up, w_down))
    for l in range(depth):
        gain = _gain_vector(q_norm_g[l], k_norm_g[l])
        qa, ka, va, qd, kd, vd, qc, kvc = _inproj(
            x2, norm1_g[l][None, :], w_in, l, gsum, gain, tabs, S)
        seq = lambda a: a.reshape(B, S, a.shape[-1])
        out_a = _nbr_attention(seq(qa), seq(ka), seq(va), _bias2(rpb[l]))
        out_d = _dilated_attention(seq(qd), seq(kd), seq(vd))
        out_c = _gqa_attention(seq(qc), seq(kvc))
        x2 = _ffn(out_a.reshape(N, NA_W), out_d.reshape(N, DIL_W), out_c.reshape(N, GQ_W), x2,
                  out_norm_g[l][None, :], w_out, norm2_g[l][None, :],
                  w_gate_up, conv_w[l], conv_b[l][None, :], w_down, l, S)
    return x2.reshape(B, S, D)
```

```python
import functools

import jax
import jax.numpy as jnp
from jax import lax
from jax.experimental import pallas as pl
from jax.experimental.pallas import tpu as pltpu

BF = jnp.bfloat16
F32 = jnp.float32

HEAD_DIM = 64
H_NA, H_DIL, H_GQ, H_GKV = 4, 6, 6, 2
NA_W, DIL_W, GQ_W, GKV_W = 256, 384, 384, 128
GRID_W = 64
NA_KH = 8
NA_KW = 16
DIL_PATTERNS = ((128, 1), (512, 4), (2048, 16))
BAND_HALF = 64
ROPE_THETA = 500000.0
ROPE_DIMS = 16
AXIAL_THETA = 10000.0
EPS = 1e-6
NEG_INF = -1e30
LOG2E = 1.4426950408889634

LANES = 128
MXU_N = 256
VMEM_LIMIT = 56 * 1024 * 1024

TM_IN = 1024
PROJ_SUB = 256
TM_FFN = 1024
FFN_HALO = 16
TQ_BAND = 128
BAND_WIN = 256
TQ_GQA = 256
TK_GQA = 2048


def _cparams(n_axes):
    return pltpu.CompilerParams(
        dimension_semantics=("arbitrary",) * n_axes, vmem_limit_bytes=VMEM_LIMIT)


def _rms(v, g):
    ms = jnp.mean(v * v, axis=-1, keepdims=True)
    return v * lax.rsqrt(ms + EPS) * g


def _resident(shape):
    nd = len(shape)
    return pl.BlockSpec(shape, lambda *_: (0,) * nd, pipeline_mode=pl.Buffered(1))


def _resident_layer(stacked_shape, l):
    nd = len(stacked_shape)
    return pl.BlockSpec((None,) + tuple(stacked_shape[1:]), lambda *_: (l,) + (0,) * (nd - 1),
                        pipeline_mode=pl.Buffered(1))


def _rotary(p, cos, sin, first, half):
    up = pltpu.roll(p, LANES - half, axis=1)
    dn = pltpu.roll(p, half, axis=1)
    return p * cos + jnp.where(first, up, dn) * sin


def _inproj_body(x_ref, g1_ref, w_ref, gs_ref, gain_ref, cb_ref, sb_ref, cc_ref, sc_ref,
                 qa_ref, ka_ref, va_ref, qd_ref, kd_ref, vd_ref, qc_ref, kvc_ref):
    lane = lax.broadcasted_iota(jnp.int32, (1, LANES), 1)
    first_b = (lane % HEAD_DIM) < (ROPE_DIMS // 2)
    first_c = (lane % (HEAD_DIM // 2)) < (HEAD_DIM // 4)
    lo_half = lane < HEAD_DIM
    dests = (
        ((qa_ref, 0), (qa_ref, 128)),
        ((ka_ref, 0), (ka_ref, 128)),
        ((qd_ref, 0), (qd_ref, 128)),
        ((qd_ref, 256), (kd_ref, 0)),
        ((kd_ref, 128), (kd_ref, 256)),
        ((qc_ref, 0), (qc_ref, 128)),
        ((qc_ref, 256), (kvc_ref, 0)),
        ((va_ref, 0), (va_ref, 128)),
        ((vd_ref, 0), (vd_ref, 128)),
        ((vd_ref, 256), (kvc_ref, 128)),
    )
    for r0 in range(0, x_ref.shape[0], PROJ_SUB):
        rows = slice(r0, r0 + PROJ_SUB)
        h = _rms(x_ref[rows, :], g1_ref[...]).astype(BF)
        n_norm = gain_ref.shape[1]
        pq = jnp.dot(h, w_ref[:, :n_norm], preferred_element_type=F32)
        pv = jnp.dot(h, w_ref[:, n_norm:], preferred_element_type=F32)
        sq = (pq * pq).astype(BF)
        for c in range(10):
            cols = slice(c * MXU_N, (c + 1) * MXU_N)
            if c < 7:
                ss = jnp.dot(sq[:, cols], gs_ref[...], preferred_element_type=F32)
                p = pq[:, cols] * lax.rsqrt(ss * (1.0 / HEAD_DIM) + EPS) * gain_ref[:, cols]
            else:
                p = pv[:, (c - 7) * MXU_N:(c - 6) * MXU_N]
            halves = [p[:, :LANES], p[:, LANES:]]
            if 2 <= c <= 4:
                halves = [_rotary(v, cb_ref[rows, :], sb_ref[rows, :], first_b, ROPE_DIMS // 2)
                          for v in halves]
            elif 5 <= c <= 6:
                halves = [_rotary(v, cc_ref[rows, :], sc_ref[rows, :], first_c, HEAD_DIM // 4)
                          for v in halves]
            for v, (ref, off) in zip(halves, dests[c]):
                if ref is kvc_ref and off == LANES:
                    ref[rows, LANES:2 * LANES] = jnp.where(lo_half, v, 1.0).astype(BF)
                    ref[rows, 2 * LANES:3 * LANES] = jnp.where(lo_half, 1.0, v).astype(BF)
                else:
                    ref[rows, off:off + LANES] = v.astype(BF)


def _inproj(x2, g1, w_perm, l, gsum, gain, tabs, S):
    N, D = x2.shape
    tm = TM_IN
    assert S % tm == 0 and tm % PROJ_SUB == 0
    nt_seq = S // tm
    row = lambda i: (i, 0)
    tab = pl.BlockSpec((tm, LANES), lambda i: (i % nt_seq, 0))
    widths = (NA_W, NA_W, NA_W, DIL_W, DIL_W, DIL_W, GQ_W, 3 * GKV_W)
    return pl.pallas_call(
        _inproj_body,
        grid=(N // tm,),
        in_specs=[
            pl.BlockSpec((tm, D), row),
            _resident(g1.shape),
            _resident_layer(w_perm.shape, l),
            _resident(gsum.shape),
            _resident(gain.shape),
            tab, tab, tab, tab,
        ],
        out_specs=[pl.BlockSpec((tm, w), row) for w in widths],
        out_shape=[jax.ShapeDtypeStruct((N, w), BF) for w in widths],
        compiler_params=_cparams(1),
        name="inproj",
    )(x2, g1, w_perm, gsum, gain, *tabs)


DIL_STRIDES = tuple(d for _, d in DIL_PATTERNS)


def _band_bias():
    r = jnp.arange(TQ_BAND)[:, None]
    c = jnp.arange(BAND_WIN)[None, :]
    masks = [jnp.where(jnp.abs(r + off - c) <= BAND_HALF, 0.0, NEG_INF).astype(F32)
             for off in (0, BAND_HALF, 2 * BAND_HALF)]
    return jnp.stack(masks)


def _dil_body(q_ref, k_ref, v_ref, bias_ref, o_ref,
              stage, stage4, kc, vac, vbc, qc, acc, mx, *, S):
    tq, win = TQ_BAND, BAND_WIN
    half = S // 2
    hf = pl.program_id(2)
    lo = lax.broadcasted_iota(jnp.int32, (tq, LANES), 1) < HEAD_DIM

    def deinterleave(dsts, g0, rows):
        n4, n16 = rows // 4, rows // 16
        for r in range(4):
            cls = stage[pl.ds(r, n4, stride=4), :]
            stage4[r * n4:(r + 1) * n4, :] = cls
            for ref, fn in dsts:
                ref[g0, r * n4:(r + 1) * n4, :] = fn(cls).astype(BF)
        for r in range(16):
            cls = stage4[pl.ds((r % 4) * n4 + r // 4, n16, stride=4), :]
            for ref, fn in dsts:
                ref[g0 + 1, r * n16:(r + 1) * n16, :] = fn(cls).astype(BF)

    def lo_of(a):
        return lax.broadcasted_iota(jnp.int32, a.shape, 1) < HEAD_DIM

    same = lambda a: a
    ones_right = lambda a: jnp.where(lo_of(a), a, 1.0)
    ones_left = lambda a: jnp.where(lo_of(a), 1.0, a)

    @pl.when(hf == 0)
    def _():
        stage[...] = k_ref[0].astype(F32)
        deinterleave([(kc, same)], 0, S)
        v = v_ref[0]
        stage[...] = v.astype(F32)
        vac[0] = ones_right(v).astype(BF)
        vbc[0] = ones_left(v).astype(BF)
        deinterleave([(vac, ones_right), (vbc, ones_left)], 1, S)

    stage[0:half] = q_ref[0].astype(F32)
    deinterleave([(qc, same)], 0, half)

    for g, d in enumerate(DIL_STRIDES):
        L = S // d
        n = half // d
        for r in range(d):
            for jj in range(n // tq):
                i0 = hf * n + jj * tq
                start = jnp.clip(i0 - BAND_HALF, 0, L - win)
                kv_rows = pl.ds(pl.multiple_of(r * L + start, BAND_HALF), win)
                if g == 0:
                    q = q_ref[0, jj * tq:(jj + 1) * tq, :]
                    kwin = k_ref[0, kv_rows, :]
                    rows = pl.ds(jj * tq, tq)
                else:
                    q = qc[g - 1, r * n + jj * tq:r * n + (jj + 1) * tq, :]
                    kwin = kc[g - 1, kv_rows, :]
                    rows = pl.ds(jj * tq * d + r, tq, stride=d)
                bias = bias_ref[(i0 - start) // BAND_HALF]
                zero = jnp.zeros_like(q)
                for hd, vref in enumerate((vac, vbc)):
                    qm = jnp.where(lo, q, zero) if hd == 0 else jnp.where(lo, zero, q)
                    s = lax.dot_general(qm, kwin, (((1,), (1,)), ((), ())),
                                        preferred_element_type=F32) + bias
                    m = jnp.max(s, axis=1, keepdims=True)
                    p = jnp.exp2(s - m).astype(BF)
                    acc[g, hd, rows, :] = jnp.dot(p, vref[g, kv_rows, :],
                                                  preferred_element_type=F32)
                    mx[g, hd, rows, :] = jnp.broadcast_to(m, (tq, LANES))

    chunk = 2 * tq
    lo_c = lax.broadcasted_iota(jnp.int32, (chunk, LANES), 1) < HEAD_DIM
    n_pat = len(DIL_STRIDES)
    for c in range(half // chunk):
        rows = slice(c * chunk, (c + 1) * chunk)
        outs = []
        for hd in range(2):
            m = [mx[g, hd, rows, :] for g in range(n_pat)]
            m_all = functools.reduce(jnp.maximum, m)
            tot = sum(jnp.exp2(m[g] - m_all) * acc[g, hd, rows, :] for g in range(n_pat))
            outs.append(tot * (1.0 / pltpu.roll(tot, HEAD_DIM, axis=1)))
        o_ref[0, rows, :] = jnp.where(lo_c, outs[0], outs[1]).astype(BF)


def _dilated_attention(qd, kd, vd):
    B, S, W = qd.shape
    assert all((w // 2) // d == BAND_HALF for w, d in DIL_PATTERNS)
    assert DIL_STRIDES == (1, 4, 16) and S // DIL_STRIDES[-1] == BAND_WIN
    assert (S // 2) % (TQ_BAND * DIL_STRIDES[-1]) == 0
    half = S // 2
    bias = _band_bias()
    n_pat = len(DIL_STRIDES)
    return pl.pallas_call(
        functools.partial(_dil_body, S=S),
        grid=(B, W // LANES, 2),
        in_specs=[
            pl.BlockSpec((1, half, LANES), lambda b, p, h: (b, h, p)),
            pl.BlockSpec((1, S, LANES), lambda b, p, h: (b, 0, p)),
            pl.BlockSpec((1, S, LANES), lambda b, p, h: (b, 0, p)),
            _resident(bias.shape),
        ],
        out_specs=pl.BlockSpec((1, half, LANES), lambda b, p, h: (b, h, p)),
        out_shape=jax.ShapeDtypeStruct((B, S, W), BF),
        scratch_shapes=[
            pltpu.VMEM((S, LANES), F32),
            pltpu.VMEM((S, LANES), F32),
            pltpu.VMEM((n_pat - 1, S, LANES), BF),
            pltpu.VMEM((n_pat, S, LANES), BF),
            pltpu.VMEM((n_pat, S, LANES), BF),
            pltpu.VMEM((n_pat - 1, half, LANES), BF),
            pltpu.VMEM((n_pat, 2, half, LANES), F32),
            pltpu.VMEM((n_pat, 2, half, LANES), F32),
        ],
        compiler_params=_cparams(3),
        name="dilated_attn",
    )(qd, kd, vd, bias)


NA_DR = 2 * NA_KH - 1
NA_DC = 2 * NA_KW - 1


def _bias2_body(rpb_ref, o_ref):
    cq = lax.broadcasted_iota(jnp.int32, (GRID_W, LANES), 0)
    lane = lax.broadcasted_iota(jnp.int32, (GRID_W, LANES), 1)
    kc = lane % GRID_W
    hi = lane >= GRID_W
    c0 = jnp.clip(cq - NA_KW // 2, 0, GRID_W - NA_KW)
    col_ok = (kc >= c0) & (kc < c0 + NA_KW)
    for h in range(H_NA):
        for d in range(NA_DR + 1):
            rows = [jnp.broadcast_to(rpb_ref[h, d + k:d + k + 1, :], (GRID_W, LANES)) for k in (0, 1)]
            a = pltpu.roll(rows[0], LANES - (NA_KW - 1), axis=1, stride=1, stride_axis=0)
            b = pltpu.roll(rows[1], GRID_W - (NA_KW - 1), axis=1, stride=1, stride_axis=0)
            ok = col_ok
            if d == 0:
                ok = col_ok & hi
            elif d == NA_DR:
                ok = col_ok & jnp.logical_not(hi)
            o_ref[h, d] = jnp.where(ok, jnp.where(hi, b, a) * LOG2E, NEG_INF)


def _bias2(rpb_l):
    h, ndr, ndc = rpb_l.shape
    rpb_pad = jnp.pad(rpb_l, ((0, 0), (1, 1), (0, LANES - ndc)))
    shape = (H_NA, NA_DR + 1, GRID_W, LANES)
    return pl.pallas_call(
        _bias2_body,
        grid=(1,),
        in_specs=[_resident(rpb_pad.shape)],
        out_specs=pl.BlockSpec(shape, lambda i: (0, 0, 0, 0)),
        out_shape=jax.ShapeDtypeStruct(shape, F32),
        compiler_params=_cparams(1),
        name="nbr_bias",
    )(rpb_pad)


NA_ROWS_PER_STEP = 32


def _nbr_window_row(r, rows):
    return jnp.clip(r - NA_KH // 2, 0, rows - NA_KH)


def _tile_bias_body(b2_ref, o_ref):
    off = pl.program_id(0)
    for h in range(H_NA):
        blocks = [b2_ref[h, 2 * kk - off + NA_KH] for kk in range(NA_KH // 2)]
        o_ref[0, h] = jnp.concatenate(blocks, axis=1)


def _tile_bias(bias2, rows):
    assert rows >= NA_KH
    win = NA_KH * GRID_W
    return pl.pallas_call(
        _tile_bias_body,
        grid=(NA_KH,),
        in_specs=[_resident(bias2.shape)],
        out_specs=pl.BlockSpec((1, H_NA, GRID_W, win), lambda v: (v, 0, 0, 0)),
        out_shape=jax.ShapeDtypeStruct((NA_KH, H_NA, GRID_W, win), F32),
        compiler_params=_cparams(1),
        name="nbr_tile_bias",
    )(bias2)


def _nbr_body(q_ref, k_ref, v_ref, bias_ref, o_ref, va, vb, *, rows):
    tq = GRID_W
    win = NA_KH * GRID_W
    S = rows * GRID_W
    jt = pl.program_id(1)

    @pl.when(jt == 0)
    def _():
        lo_s = lax.broadcasted_iota(jnp.int32, (S, LANES), 1) < HEAD_DIM
        for hp in range(H_NA // 2):
            v = v_ref[0, :, hp * LANES:(hp + 1) * LANES]
            va[hp] = jnp.where(lo_s, v, jnp.ones_like(v))
            vb[hp] = jnp.where(lo_s, jnp.ones_like(v), v)

    lo = lax.broadcasted_iota(jnp.int32, (tq, LANES), 1) < HEAD_DIM
    jobs = []
    for t in range(NA_ROWS_PER_STEP):
        r = jt * NA_ROWS_PER_STEP + t
        r0 = _nbr_window_row(r, rows)
        start = pl.multiple_of(r0 * GRID_W, GRID_W)
        var = r - r0
        for hp in range(H_NA // 2):
            jobs.append((t, hp, start, var))
    scores = []
    for t, hp, start, var in jobs:
        cols = slice(hp * LANES, (hp + 1) * LANES)
        q = q_ref[0, t * tq:(t + 1) * tq, cols]
        zero = jnp.zeros_like(q)
        qs = jnp.concatenate([jnp.where(lo, q, zero), jnp.where(lo, zero, q)], axis=0)
        kw = k_ref[0, pl.ds(start, win), cols]
        bias = jnp.concatenate([bias_ref[var, 2 * hp], bias_ref[var, 2 * hp + 1]], axis=0)
        scores.append(lax.dot_general(qs, kw, (((1,), (1,)), ((), ())),
                                      preferred_element_type=F32) + bias)
    probs = [jnp.exp2(s - jnp.max(s, axis=1, keepdims=True)).astype(BF) for s in scores]
    for (t, hp, start, var), p in zip(jobs, probs):
        outs = []
        for hh, vref in enumerate((va, vb)):
            a = jnp.dot(p[hh * tq:(hh + 1) * tq], vref[hp, pl.ds(start, win), :],
                        preferred_element_type=F32)
            outs.append(a * (1.0 / pltpu.roll(a, HEAD_DIM, axis=1)))
        o_ref[0, t * tq:(t + 1) * tq, hp * LANES:(hp + 1) * LANES] = (
            jnp.where(lo, outs[0], outs[1]).astype(BF))


def _nbr_attention(qa, ka, va, bias2):
    B, S, W = qa.shape
    rows = S // GRID_W
    tq = GRID_W * NA_ROWS_PER_STEP
    assert S % tq == 0
    bias_full = _tile_bias(bias2, rows)
    qblk = pl.BlockSpec((1, tq, W), lambda b, j: (b, j, 0))
    kblk = pl.BlockSpec((1, S, W), lambda b, j: (b, 0, 0))
    return pl.pallas_call(
        functools.partial(_nbr_body, rows=rows),
        grid=(B, S // tq),
        in_specs=[qblk, kblk, kblk, _resident(bias_full.shape)],
        out_specs=qblk,
        scratch_shapes=[pltpu.VMEM((H_NA // 2, S, LANES), BF) for _ in range(2)],
        out_shape=jax.ShapeDtypeStruct((B, S, W), BF),
        compiler_params=_cparams(2),
        name="nbr_attn",
    )(qa, ka, va, bias_full)


def _gqa_body(q_ref, kv_ref, o_ref, *, S):
    tq, tk = TQ_GQA, TK_GQA
    g = H_GQ // H_GKV
    lane = lax.broadcasted_iota(jnp.int32, (tq, LANES), 1)
    lo = lane < HEAD_DIM
    qs = []
    for kvh in range(H_GKV):
        mine = lo if kvh == 0 else jnp.logical_not(lo)
        parts = []
        for i in range(g):
            h = g * kvh + i
            chunk = q_ref[0, :, (h // 2) * LANES:(h // 2 + 1) * LANES]
            if (h % 2) != kvh:
                chunk = pltpu.roll(chunk.astype(F32), HEAD_DIM, axis=1).astype(BF)
            parts.append(jnp.where(mine, chunk, jnp.zeros_like(chunk)))
        qs.append(jnp.concatenate(parts, axis=0))

    def step(c, carry):
        rows = pl.ds(pl.multiple_of(c * tk, tk), tk)
        kblk = kv_ref[0, rows, 0:LANES]
        new = []
        for kvh in range(H_GKV):
            m, acc = carry[kvh]
            vblk = kv_ref[0, rows, (1 + kvh) * LANES:(2 + kvh) * LANES]
            s = lax.dot_general(qs[kvh], kblk, (((1,), (1,)), ((), ())),
                                preferred_element_type=F32)
            m_new = jnp.maximum(m, jnp.max(s, axis=1, keepdims=True))
            p = jnp.exp2(s - m_new)
            acc = jnp.exp2(m - m_new) * acc + jnp.dot(p.astype(BF), vblk,
                                                      preferred_element_type=F32)
            new.append((m_new, acc))
        return tuple(new)

    init = tuple((jnp.full((g * tq, 1), NEG_INF, F32), jnp.zeros((g * tq, LANES), F32))
                 for _ in range(H_GKV))
    fin = lax.fori_loop(0, S // tk, step, init, unroll=True)
    res = [acc * (1.0 / pltpu.roll(acc, HEAD_DIM, axis=1)) for _, acc in fin]
    for c in range(GQ_W // LANES):
        halves = []
        for hh in range(2):
            h = 2 * c + hh
            kvh, i = h // g, h % g
            v = res[kvh][i * tq:(i + 1) * tq]
            if kvh != hh:
                v = pltpu.roll(v, HEAD_DIM, axis=1)
            halves.append(v)
        o_ref[0, :, c * LANES:(c + 1) * LANES] = jnp.where(lo, halves[0], halves[1]).astype(BF)


def _gqa_attention(qc, kvc):
    B, S, W = qc.shape
    qblk = pl.BlockSpec((1, TQ_GQA, W), lambda b, j: (b, j, 0))
    kvblk = pl.BlockSpec((1, S, kvc.shape[-1]), lambda b, j: (b, 0, 0))
    return pl.pallas_call(
        functools.partial(_gqa_body, S=S),
        grid=(B, S // TQ_GQA),
        in_specs=[qblk, kvblk],
        out_specs=qblk,
        out_shape=jax.ShapeDtypeStruct((B, S, W), BF),
        compiler_params=_cparams(2),
        name="gqa_attn",
    )(qc, kvc)


def _ffn_body(*refs, tiles_per_seq, d_ff):
    (oa, oa_p, oa_n, od, od_p, od_n, oc, oc_p, oc_n, x, x_p, x_n,
     og_ref, wo_ref, g2_ref, wgu_ref, cw_ref, cb_ref, wd_ref, o_ref, hext, gs, act) = refs
    tm, halo = TM_FFN, FFN_HALO
    i = pl.program_id(0) % tiles_per_seq

    def ext(main, prev, nxt):
        return jnp.concatenate([prev[...], main[...], nxt[...]], axis=0)

    na = _rms(ext(oa, oa_p, oa_n).astype(F32), og_ref[:, :NA_W])
    nd = _rms(ext(od, od_p, od_n).astype(F32), og_ref[:, NA_W:NA_W + DIL_W])
    nc = _rms(ext(oc, oc_p, oc_n).astype(F32), og_ref[:, NA_W + DIL_W:])
    mix = jnp.concatenate([na, nd, nc], axis=1).astype(BF)
    y_ext = ext(x, x_p, x_n) + jnp.dot(mix, wo_ref[...], preferred_element_type=F32)
    y = y_ext[halo:halo + tm]
    row = lax.broadcasted_iota(jnp.int32, (tm + 2 * halo, 1), 0)
    keep = ((row >= halo) | (i > 0)) & ((row < halo + tm) | (i < tiles_per_seq - 1))
    hext[...] = jnp.where(keep, _rms(y_ext, g2_ref[...]), 0.0).astype(BF)
    for c in range(d_ff // MXU_N):
        cols = slice(c * MXU_N, (c + 1) * MXU_N)
        ucols = slice(d_ff + c * MXU_N, d_ff + (c + 1) * MXU_N)
        gs[...] = jnp.dot(hext[...], wgu_ref[:, cols], preferred_element_type=F32)
        u = jnp.dot(hext[halo:halo + tm], wgu_ref[:, ucols], preferred_element_type=F32)
        gc = gs[halo - 1:halo - 1 + tm] * cw_ref[0:1, cols] + cb_ref[:, cols]
        gc = gc + gs[halo:halo + tm] * cw_ref[1:2, cols]
        gc = gc + gs[halo + 1:halo + 1 + tm] * cw_ref[2:3, cols]
        gelu = 0.5 * gc * (1.0 + lax.erf(gc * (0.5 ** 0.5)))
        act[:, cols] = (gelu * u).astype(BF)
    o_ref[...] = y + jnp.dot(act[...], wd_ref[...], preferred_element_type=F32)


def _ffn(oa, od, oc, x2, og, w_out, g2, wgu, cw, cb, wd, l, S):
    N, D = x2.shape
    d_ff = wd.shape[1]
    tm, halo = TM_FFN, FFN_HALO
    assert S % tm == 0 and d_ff % MXU_N == 0
    r = tm // halo
    nblk = N // halo
    row = lambda i: (i, 0)
    prev = lambda i: (jnp.maximum(i * r - 1, 0), 0)
    nxt = lambda i: (jnp.minimum((i + 1) * r, nblk - 1), 0)
    tiled, args = [], []
    for a in (oa, od, oc, x2):
        w = a.shape[1]
        tiled += [pl.BlockSpec((tm, w), row), pl.BlockSpec((halo, w), prev),
                  pl.BlockSpec((halo, w), nxt)]
        args += [a, a, a]
    params = (og, w_out, g2, wgu, cw, cb, wd)
    return pl.pallas_call(
        functools.partial(_ffn_body, tiles_per_seq=S // tm, d_ff=d_ff),
        grid=(N // tm,),
        in_specs=tiled + [_resident_layer(p.shape, l) if p.ndim == 3 else _resident(p.shape)
                          for p in params],
        out_specs=pl.BlockSpec((tm, D), row),
        out_shape=jax.ShapeDtypeStruct((N, D), F32),
        scratch_shapes=[
            pltpu.VMEM((tm + 2 * halo, D), BF),
            pltpu.VMEM((tm + 2 * halo, MXU_N), F32),
            pltpu.VMEM((tm, d_ff), BF),
        ],
        compiler_params=_cparams(1),
        name="ffn",
    )(*args, *params)


def _rope_cos_sin(pos, dim, theta):
    inv = theta ** (-jnp.arange(0, dim, 2, dtype=F32) / dim)
    ang = inv[:, None] * pos.astype(F32)[None, :]
    cos, sin = lax.optimization_barrier((jnp.cos(ang), jnp.sin(ang)))
    return cos.T, sin.T


def _rotary_tables(S):
    t = jnp.arange(S, dtype=jnp.int32)
    cos1, sin1 = _rope_cos_sin(t, ROPE_DIMS, ROPE_THETA)
    rest = HEAD_DIM - ROPE_DIMS
    cb = jnp.concatenate([cos1, cos1, jnp.ones((S, rest), F32)], axis=1)
    sb = jnp.concatenate([-sin1, sin1, jnp.zeros((S, rest), F32)], axis=1)
    cg, sg = _rope_cos_sin(jnp.arange(GRID_W, dtype=jnp.int32), HEAD_DIM // 2, AXIAL_THETA)
    cr, sr = (jnp.repeat(a, S // GRID_W, axis=0) for a in (cg, sg))
    cc, sc = (jnp.tile(a, (S // GRID_W, 1)) for a in (cg, sg))
    ca = jnp.concatenate([cr, cr, cc, cc], axis=1)
    sa = jnp.concatenate([-sr, sr, -sc, sc], axis=1)
    two = lambda a: jnp.concatenate([a, a], axis=1)
    return two(cb), two(sb), two(ca), two(sa)


def _permute_in_columns(w):
    sizes = (NA_W, NA_W, NA_W, DIL_W, DIL_W, DIL_W, GQ_W, GKV_W, GKV_W)
    offs = [0]
    for s in sizes:
        offs.append(offs[-1] + s)
    seg = [w[..., offs[i]:offs[i + 1]] for i in range(len(sizes))]
    qa, ka, va, qd, kd, vd, qc, kc, vc = seg
    return jnp.concatenate([qa, ka, qd, kd, qc, kc, va, vd, vc], axis=-1)


def _gain_vector(qg, kg):
    scale = HEAD_DIM ** -0.5
    parts = [
        jnp.tile(qg[0] * (scale * LOG2E), H_NA), jnp.tile(kg[0], H_NA),
        jnp.tile(qg[1] * (scale * LOG2E), H_DIL), jnp.tile(kg[1], H_DIL),
        jnp.tile(qg[2] * (scale * LOG2E), H_GQ), jnp.tile(kg[2], H_GKV),
    ]
    return jnp.concatenate(parts)[None, :]


def kernel(x, norm1_g, w_in, q_norm_g, k_norm_g, rpb, out_norm_g, w_out, norm2_g,
           w_gate_up, conv_w, conv_b, w_down):
    B, S, D = x.shape
    depth = w_in.shape[0]
    N = B * S
    assert S == GRID_W * GRID_W and S % TQ_GQA == 0 and S % TK_GQA == 0
    tabs = _rotary_tables(S)
    idx = jnp.arange(MXU_N) // HEAD_DIM
    gsum = (idx[:, None] == idx[None, :]).astype(BF)
    x2 = x.reshape(N, D)
    w_perm = _permute_in_columns(w_in.astype(BF))
    w_out, w_gate_up, w_down = (w.astype(BF) for w in (w_out, w_gate_up, w_down))
    for l in range(depth):
        gain = _gain_vector(q_norm_g[l], k_norm_g[l])
        qa, ka, va, qd, kd, vd, qc, kvc = _inproj(
            x2, norm1_g[l][None, :], w_perm, l, gsum, gain, tabs, S)
        seq = lambda a: a.reshape(B, S, a.shape[-1])
        out_a = _nbr_attention(seq(qa), seq(ka), seq(va), _bias2(rpb[l]))
        out_d = _dilated_attention(seq(qd), seq(kd), seq(vd))
        out_c = _gqa_attention(seq(qc), seq(kvc))
        x2 = _ffn(out_a.reshape(N, NA_W), out_d.reshape(N, DIL_W), out_c.reshape(N, GQ_W), x2,
                  out_norm_g[l][None, :], w_out, norm2_g[l][None, :],
                  w_gate_up, conv_w[l], conv_b[l][None, :], w_down, l, S)
    return x2.reshape(B, S, D)
```
